```python
import jax, jax.numpy as jnp
from jax import lax
import numpy as np

D_MODEL = 2048
BATCH = 8
SEQ = 2048
DEPTH = 1

CONV_DIM = D_MODEL // 2
CONV_WIDTH = 31
RET_HEADS = 8
RET_QK_DIM = D_MODEL // 2
RET_V_DIM = D_MODEL
QK_HEAD = RET_QK_DIM // RET_HEADS
V_HEAD = RET_V_DIM // RET_HEADS
CHUNK = 128
D_FF = 4 * D_MODEL
ROPE_BASE = 10000.0
EPS = 1e-6
N_BRANCH = 2
IN_COLS = 2 * CONV_DIM + 2 * RET_QK_DIM + 2 * RET_V_DIM + N_BRANCH * D_MODEL
N_MOD = 6

kernel_name = "hybrid_conformer_retention_adaln_block"


def rmsnorm(x, g):
    xf = x.astype(jnp.float32)
    y = xf * lax.rsqrt(jnp.mean(xf * xf, axis=-1, keepdims=True) + EPS)
    return (y * g.astype(jnp.float32)).astype(x.dtype)


def layernorm(x, g, b):
    xf = x.astype(jnp.float32)
    mu = jnp.mean(xf, axis=-1, keepdims=True)
    var = jnp.mean(jnp.square(xf - mu), axis=-1, keepdims=True)
    y = (xf - mu) * lax.rsqrt(var + EPS)
    return (y * g.astype(jnp.float32) + b.astype(jnp.float32)).astype(x.dtype)


def modulate(h, shift, scale):
    return h * (1.0 + scale[:, None, :]) + shift[:, None, :]


def rotary(x, positions):
    dh = x.shape[-1]
    half = dh // 2
    inv_freq = ROPE_BASE ** (-jnp.arange(0, half, dtype=jnp.float32) / half)
    ang = positions.astype(jnp.float32)[..., None] * inv_freq
    cos = jnp.cos(ang)[:, :, None, :]
    sin = jnp.sin(ang)[:, :, None, :]
    x1 = x[..., :half].astype(jnp.float32)
    x2 = x[..., half:].astype(jnp.float32)
    out = jnp.concatenate([x1 * cos - x2 * sin, x1 * sin + x2 * cos], axis=-1)
    return out.astype(x.dtype)


def conformer_conv_branch(u, conv_w, conv_b, ln_g, ln_b, w_proj):
    a, gt = u[..., :CONV_DIM], u[..., CONV_DIM:]
    v = a * jax.nn.sigmoid(gt)
    y = lax.conv_general_dilated(
        v, conv_w[:, None, :].astype(v.dtype), window_strides=(1,),
        padding=[(CONV_WIDTH - 1, 0)],
        dimension_numbers=("NWC", "WIO", "NWC"),
        feature_group_count=CONV_DIM) + conv_b
    y = jax.nn.silu(layernorm(y, ln_g, ln_b))
    return y @ w_proj


def retention_branch(q, k, v, g, positions, gn_g, gn_b, w_proj):
    B, S, _ = q.shape
    N = S // CHUNK
    q = rotary(q.reshape(B, S, RET_HEADS, QK_HEAD), positions)
    k = rotary(k.reshape(B, S, RET_HEADS, QK_HEAD), positions) * (QK_HEAD ** -0.5)
    v = v.reshape(B, S, RET_HEADS, V_HEAD)
    qc = q.reshape(B, N, CHUNK, RET_HEADS, QK_HEAD)
    kc = k.reshape(B, N, CHUNK, RET_HEADS, QK_HEAD)
    vc = v.reshape(B, N, CHUNK, RET_HEADS, V_HEAD)

    log_gamma = jnp.log(1.0 - jnp.exp2(-5.0 - jnp.arange(RET_HEADS, dtype=jnp.float32)))
    idx = jnp.arange(CHUNK, dtype=jnp.float32)
    diff = idx[:, None] - idx[None, :]
    decay_mask = jnp.where(diff[None] >= 0,
                           jnp.exp(jnp.maximum(diff, 0.0)[None] * log_gamma[:, None, None]),
                           0.0)
    zeta = jnp.exp((CHUNK - 1 - idx)[None, :] * log_gamma[:, None])
    xi = jnp.exp((idx + 1.0)[None, :] * log_gamma[:, None])
    chunk_decay = jnp.exp(CHUNK * log_gamma)

    scores = jnp.einsum("bnqhd,bnkhd->bnhqk", qc, kc) * decay_mask[None, None].astype(qc.dtype)
    inner = jnp.einsum("bnhqk,bnkhv->bnqhv", scores, vc)

    kv_chunk = jnp.einsum("bnkhd,hk,bnkhv->bnhdv", kc, zeta.astype(kc.dtype), vc)
    kv_seq = jnp.moveaxis(kv_chunk, 1, 0)
    cd = chunk_decay.astype(kv_seq.dtype)[None, :, None, None]

    def step(R, kv):
        return cd * R + kv, R

    _, states = lax.scan(step, jnp.zeros_like(kv_seq[0]), kv_seq)
    states = jnp.moveaxis(states, 0, 1)
    cross = jnp.einsum("bnqhd,hq,bnhdv->bnqhv", qc, xi.astype(qc.dtype), states)

    o = (inner + cross).reshape(B, S, RET_HEADS, V_HEAD)
    of = o.astype(jnp.float32)
    mu = jnp.mean(of, axis=-1, keepdims=True)
    var = jnp.mean(jnp.square(of - mu), axis=-1, keepdims=True)
    on = ((of - mu) * lax.rsqrt(var + EPS)).reshape(B, S, RET_V_DIM)
    on = (on * gn_g.astype(jnp.float32) + gn_b.astype(jnp.float32)).astype(q.dtype)
    return (jax.nn.silu(g) * on) @ w_proj


def setup_inputs(seed: int = 0) -> dict:
    key = jax.random.key(seed)
    ks = jax.random.split(key, 20)
    f32 = jnp.float32
    D = D_MODEL

    def nrm(k, shape, fan_in, scale=1.0):
        return jax.random.normal(k, shape, f32) * (scale * fan_in ** -0.5)

    x = jax.random.normal(ks[0], (BATCH, SEQ, D), f32)
    c = jax.random.normal(ks[1], (BATCH, D), f32)
    positions = jnp.broadcast_to(jnp.arange(SEQ, dtype=jnp.int32), (BATCH, SEQ))
    return {
        "x": x,
        "c": c,
        "positions": positions,
        "w_ada": nrm(ks[2], (DEPTH, D, N_MOD * D), D, 0.5),
        "b_ada": 0.01 * jax.random.normal(ks[3], (DEPTH, N_MOD * D), f32),
        "g_norm_mix": 1.0 + 0.02 * jax.random.normal(ks[4], (DEPTH, D), f32),
        "w_in": nrm(ks[5], (DEPTH, D, IN_COLS), D),
        "conv_w": nrm(ks[6], (DEPTH, CONV_WIDTH, CONV_DIM), CONV_WIDTH),
        "conv_b": 0.01 * jax.random.normal(ks[7], (DEPTH, CONV_DIM), f32),
        "conv_ln_g": 1.0 + 0.02 * jax.random.normal(ks[8], (DEPTH, CONV_DIM), f32),
        "conv_ln_b": 0.01 * jax.random.normal(ks[9], (DEPTH, CONV_DIM), f32),
        "w_conv_out": nrm(ks[10], (DEPTH, CONV_DIM, D), CONV_DIM),
        "ret_gn_g": 1.0 + 0.02 * jax.random.normal(ks[11], (DEPTH, RET_V_DIM), f32),
        "ret_gn_b": 0.01 * jax.random.normal(ks[12], (DEPTH, RET_V_DIM), f32),
        "w_ret_out": nrm(ks[13], (DEPTH, RET_V_DIM, D), RET_V_DIM),
        "w_out": nrm(ks[14], (DEPTH, D, D), D),
        "g_norm_ffn": 1.0 + 0.02 * jax.random.normal(ks[15], (DEPTH, D), f32),
        "w_ff1": nrm(ks[16], (DEPTH, D, D_FF), D),
        "w_ff2": nrm(ks[17], (DEPTH, D_FF, D), D_FF),
        "g_norm_final": 1.0 + 0.02 * jax.random.normal(ks[18], (D,), f32),
    }


def reference(x, c, positions, w_ada, b_ada, g_norm_mix, w_in, conv_w, conv_b,
              conv_ln_g, conv_ln_b, w_conv_out, ret_gn_g, ret_gn_b, w_ret_out,
              w_out, g_norm_ffn, w_ff1, w_ff2, g_norm_final):
    D = D_MODEL
    c_act = jax.nn.silu(c)
    o_conv = 2 * CONV_DIM
    o_q = o_conv + RET_QK_DIM
    o_k = o_q + RET_QK_DIM
    o_v = o_k + RET_V_DIM
    o_g = o_v + RET_V_DIM
    o_ga = o_g + D
    for l in range(DEPTH):
        mod = c_act @ w_ada[l] + b_ada[l]
        shift_m, scale_m, gate_m = mod[:, :D], mod[:, D:2 * D], mod[:, 2 * D:3 * D]
        shift_f, scale_f, gate_f = mod[:, 3 * D:4 * D], mod[:, 4 * D:5 * D], mod[:, 5 * D:]

        h = modulate(rmsnorm(x, g_norm_mix[l]), shift_m, scale_m)
        proj = h @ w_in[l]
        y_conv = conformer_conv_branch(proj[..., :o_conv], conv_w[l], conv_b[l],
                                       conv_ln_g[l], conv_ln_b[l], w_conv_out[l])
        y_ret = retention_branch(proj[..., o_conv:o_q], proj[..., o_q:o_k],
                                 proj[..., o_k:o_v], proj[..., o_v:o_g], positions,
                                 ret_gn_g[l], ret_gn_b[l], w_ret_out[l])
        gate_conv = jax.nn.sigmoid(proj[..., o_g:o_ga])
        gate_ret = jax.nn.sigmoid(proj[..., o_ga:])
        merged = gate_conv * y_conv + gate_ret * y_ret
        x = x + gate_m[:, None, :] * (merged @ w_out[l])

        h = modulate(rmsnorm(x, g_norm_ffn[l]), shift_f, scale_f)
        ff = jnp.square(jax.nn.relu(h @ w_ff1[l])) @ w_ff2[l]
        x = x + gate_f[:, None, :] * ff
    return rmsnorm(x, g_norm_final)
```

```python
import functools

import jax
import jax.numpy as jnp
from jax import lax
from jax.experimental import pallas as pl
from jax.experimental.pallas import tpu as pltpu

D_MODEL = 2048
CONV_DIM = D_MODEL // 2
CONV_WIDTH = 31
RET_HEADS = 8
RET_QK_DIM = D_MODEL // 2
RET_V_DIM = D_MODEL
QK_HEAD = RET_QK_DIM // RET_HEADS
V_HEAD = RET_V_DIM // RET_HEADS
D_FF = 4 * D_MODEL
ROPE_BASE = 10000.0
EPS = 1e-6
IN_COLS = 2 * CONV_DIM + 2 * RET_QK_DIM + 2 * RET_V_DIM + 2 * D_MODEL
N_MOD = 6

RET_CHUNK = 256
CONV_HIST = 32
CONV_ROWS = 128
CONV_LANES = 128

VMEM_LIMIT = 56 * 1024 * 1024

BF16 = jnp.bfloat16
F32 = jnp.float32


def _params(sem, vmem=VMEM_LIMIT):
    return pltpu.CompilerParams(dimension_semantics=sem, vmem_limit_bytes=vmem)


def _mod_kernel(c_ref, w_ref, b_ref, o_ref):
    c = c_ref[...]
    ca = (c * jax.nn.sigmoid(c)).astype(BF16)
    o_ref[...] = jnp.dot(ca, w_ref[...].astype(BF16), preferred_element_type=F32) + b_ref[...]


def _mod(c, w_ada, b_ada):
    B, D = c.shape
    N = w_ada.shape[1]
    tn = 1024
    return pl.pallas_call(
        _mod_kernel,
        out_shape=jax.ShapeDtypeStruct((B, N), F32),
        grid=(N // tn,),
        in_specs=[
            pl.BlockSpec((B, D), lambda j: (0, 0)),
            pl.BlockSpec((D, tn), lambda j: (0, j)),
            pl.BlockSpec((1, tn), lambda j: (0, j)),
        ],
        out_specs=pl.BlockSpec((B, tn), lambda j: (0, j)),
        compiler_params=_params(("arbitrary",)),
        name="mod",
    )(c, w_ada, b_ada.reshape(1, N))


def _rms_modulate(x, g, shift, scale):
    ms = jnp.mean(x * x, axis=-1, keepdims=True)
    y = x * lax.rsqrt(ms + EPS) * g
    return y * (1.0 + scale) + shift


def _inproj_kernel(x_ref, g_ref, sh_ref, sc_ref, w_ref, o_ref, h_ref):
    @pl.when(pl.program_id(1) == 0)
    def _():
        h_ref[...] = _rms_modulate(x_ref[...], g_ref[...], sh_ref[0], sc_ref[0]).astype(BF16)

    o_ref[...] = jnp.dot(h_ref[...], w_ref[...], preferred_element_type=F32).astype(o_ref.dtype)


def _inproj(x2d, g, shift, scale, w, seq):
    M, D = x2d.shape
    N = w.shape[1]
    tm, tn = 1024, 1024
    per_b = seq // tm
    return pl.pallas_call(
        _inproj_kernel,
        out_shape=jax.ShapeDtypeStruct((M, N), BF16),
        grid=(M // tm, N // tn),
        in_specs=[
            pl.BlockSpec((tm, D), lambda i, j: (i, 0)),
            pl.BlockSpec((1, D), lambda i, j: (0, 0)),
            pl.BlockSpec((1, 1, D), lambda i, j: (i // per_b, 0, 0)),
            pl.BlockSpec((1, 1, D), lambda i, j: (i // per_b, 0, 0)),
            pl.BlockSpec((D, tn), lambda i, j: (0, j)),
        ],
        out_specs=pl.BlockSpec((tm, tn), lambda i, j: (i, j)),
        scratch_shapes=[pltpu.VMEM((tm, D), BF16)],
        compiler_params=_params(("parallel", "arbitrary")),
        name="in_proj",
    )(x2d, g, shift, scale, w)


def _conv_kernel(a_ref, gt_ref, w_ref, b_ref, lg_ref, lb_ref, o_ref, vext_ref, y_ref, *, ts, rows):
    s = pl.program_id(1)

    @pl.when(s == 0)
    def _():
        vext_ref[0:CONV_HIST, :] = jnp.zeros((CONV_HIST, CONV_DIM), F32)

    @pl.when(s != 0)
    def _():
        vext_ref[0:CONV_HIST, :] = vext_ref[ts:ts + CONV_HIST, :]

    a = a_ref[...].astype(F32)
    gt = gt_ref[...].astype(F32)
    vext_ref[CONV_HIST:CONV_HIST + ts, :] = a * jax.nn.sigmoid(gt)

    SUB = 8
    R, LC = CONV_ROWS, CONV_LANES

    def conv_body(r, carry):
        r0 = pl.multiple_of(r * R, R)
        for c in range(CONV_DIM // LC):
            ls = slice(c * LC, (c + 1) * LC)
            acc = None
            for b in range(SUB):
                p = None
                for a in range(-(-CONV_WIDTH // SUB)):
                    j = SUB * a + b
                    if j >= CONV_WIDTH:
                        continue
                    start = pl.multiple_of(r0 + (CONV_HIST - SUB - SUB * a), SUB)
                    k = CONV_WIDTH - 1 - j
                    term = vext_ref[pl.ds(start, R + SUB), ls] * w_ref[k:k + 1, ls]
                    p = term if p is None else p + term
                if b:
                    p = pltpu.roll(p, b, 0)
                p = p[SUB:]
                acc = p if acc is None else acc + p
            y_ref[pl.ds(r0, R), ls] = acc + b_ref[:, ls]
        return carry

    lax.fori_loop(0, ts // R, conv_body, 0)

    lg = lg_ref[...]
    lb = lb_ref[...]

    def ln_body(r, carry):
        r0 = pl.multiple_of(r * rows, rows)
        y = y_ref[pl.ds(r0, rows), :]
        mu = jnp.mean(y, axis=-1, keepdims=True)
        yc = y - mu
        var = jnp.mean(yc * yc, axis=-1, keepdims=True)
        yn = yc * lax.rsqrt(var + EPS) * lg + lb
        o_ref[pl.ds(r0, rows), :] = (yn * jax.nn.sigmoid(yn)).astype(o_ref.dtype)
        return carry

    lax.fori_loop(0, ts // rows, ln_body, 0)


def _conv(proj, conv_w, conv_b, ln_g, ln_b, batch, seq):
    M = proj.shape[0]
    ts, rows = 512, 16
    per_b = seq // ts
    C = CONV_DIM
    return pl.pallas_call(
        functools.partial(_conv_kernel, ts=ts, rows=rows),
        out_shape=jax.ShapeDtypeStruct((M, C), BF16),
        grid=(batch, per_b),
        in_specs=[
            pl.BlockSpec((ts, C), lambda b, s: (b * per_b + s, 0)),
            pl.BlockSpec((ts, C), lambda b, s: (b * per_b + s, 1)),
            pl.BlockSpec((CONV_WIDTH, C), lambda b, s: (0, 0)),
            pl.BlockSpec((1, C), lambda b, s: (0, 0)),
            pl.BlockSpec((1, C), lambda b, s: (0, 0)),
            pl.BlockSpec((1, C), lambda b, s: (0, 0)),
        ],
        out_specs=pl.BlockSpec((ts, C), lambda b, s: (b * per_b + s, 0)),
        scratch_shapes=[pltpu.VMEM((CONV_HIST + ts, C), F32), pltpu.VMEM((ts, C), F32)],
        compiler_params=_params(("parallel", "arbitrary")),
        name="conv_branch",
    )(proj, proj, conv_w, conv_b.reshape(1, C), ln_g.reshape(1, C), ln_b.reshape(1, C))


def _ret_kernel(pos_ref, q_ref, k_ref, v_ref, g_ref, invf_ref, sgn_ref, mask_ref, xi_ref, zeta_ref,
                cd_ref, gng_ref, gnb_ref, o_ref, r_ref, *, ts):
    s = pl.program_id(1)

    @pl.when(s == 0)
    def _():
        r_ref[...] = jnp.zeros(r_ref.shape, F32)

    ang = pos_ref[...].astype(F32) * invf_ref[...]
    cosf = jnp.cos(ang)
    sinf = jnp.sin(ang) * sgn_ref[...]
    C = RET_CHUNK
    for n in range(ts // C):
        rs = slice(n * C, (n + 1) * C)
        cs_, sn_ = cosf[rs], sinf[rs]
        for h in range(RET_HEADS):
            qs = slice(h * QK_HEAD, (h + 1) * QK_HEAD)
            vs = slice(h * V_HEAD, (h + 1) * V_HEAD)
            q = q_ref[rs, qs].astype(F32)
            k = k_ref[rs, qs].astype(F32)
            q = q * cs_ + pltpu.roll(q, QK_HEAD // 2, 1) * sn_
            k = (k * cs_ + pltpu.roll(k, QK_HEAD // 2, 1) * sn_) * (QK_HEAD ** -0.5)
            v = v_ref[rs, vs]
            qb = q.astype(BF16)
            kb = k.astype(BF16)
            sc = lax.dot_general(qb, kb, (((1,), (1,)), ((), ())), preferred_element_type=F32)
            sc = sc * mask_ref[h]
            inner = jnp.dot(sc.astype(BF16), v, preferred_element_type=F32)
            R = r_ref[h]
            qx = (q * xi_ref[h]).astype(BF16)
            cross = jnp.dot(qx, R.astype(BF16), preferred_element_type=F32)
            kz = (k * zeta_ref[h]).astype(BF16)
            kv = lax.dot_general(kz, v, (((0,), (0,)), ((), ())), preferred_element_type=F32)
            r_ref[h] = R * cd_ref[h] + kv
            o = inner + cross
            mu = jnp.mean(o, axis=-1, keepdims=True)
            oc = o - mu
            var = jnp.mean(oc * oc, axis=-1, keepdims=True)
            on = oc * lax.rsqrt(var + EPS) * gng_ref[:, vs] + gnb_ref[:, vs]
            g = g_ref[rs, vs].astype(F32)
            o_ref[rs, vs] = (g * jax.nn.sigmoid(g) * on).astype(o_ref.dtype)


def _ret_tables():
    C = RET_CHUNK
    log_gamma = jnp.log(1.0 - jnp.exp2(-5.0 - jnp.arange(RET_HEADS, dtype=F32)))
    idx = jnp.arange(C, dtype=F32)
    diff = idx[:, None] - idx[None, :]
    mask = jnp.where(diff[None] >= 0,
                     jnp.exp(jnp.maximum(diff, 0.0)[None] * log_gamma[:, None, None]), 0.0)
    zeta = jnp.exp((C - 1 - idx)[None, :] * log_gamma[:, None])
    xi = jnp.exp((idx + 1.0)[None, :] * log_gamma[:, None])
    cd = jnp.exp(C * log_gamma)
    half = QK_HEAD // 2
    inv_freq = ROPE_BASE ** (-jnp.arange(0, half, dtype=F32) / half)
    invf = jnp.concatenate([inv_freq, inv_freq]).reshape(1, QK_HEAD)
    sgn = jnp.concatenate([-jnp.ones((half,), F32), jnp.ones((half,), F32)]).reshape(1, QK_HEAD)
    cd_b = jnp.broadcast_to(cd[:, None, None], (RET_HEADS, 1, V_HEAD))
    return invf, sgn, mask, xi[:, :, None], zeta[:, :, None], cd_b


def _ret(proj, positions, gn_g, gn_b, batch, seq):
    M = proj.shape[0]
    ts = 512
    per_b = seq // ts
    C = RET_CHUNK
    invf, sgn, mask, xi, zeta, cd = _ret_tables()
    row = lambda b, s: b * per_b + s
    const2 = lambda b, s: (0, 0)
    const3 = lambda b, s: (0, 0, 0)
    return pl.pallas_call(
        functools.partial(_ret_kernel, ts=ts),
        out_shape=jax.ShapeDtypeStruct((M, RET_V_DIM), BF16),
        grid=(batch, per_b),
        in_specs=[
            pl.BlockSpec((ts, 1), lambda b, s: (row(b, s), 0)),
            pl.BlockSpec((ts, RET_QK_DIM), lambda b, s: (row(b, s), 2)),
            pl.BlockSpec((ts, RET_QK_DIM), lambda b, s: (row(b, s), 3)),
            pl.BlockSpec((ts, RET_V_DIM), lambda b, s: (row(b, s), 2)),
            pl.BlockSpec((ts, RET_V_DIM), lambda b, s: (row(b, s), 3)),
            pl.BlockSpec((1, QK_HEAD), const2),
            pl.BlockSpec((1, QK_HEAD), const2),
            pl.BlockSpec((RET_HEADS, C, C), const3),
            pl.BlockSpec((RET_HEADS, C, 1), const3),
            pl.BlockSpec((RET_HEADS, C, 1), const3),
            pl.BlockSpec((RET_HEADS, 1, V_HEAD), const3),
            pl.BlockSpec((1, RET_V_DIM), const2),
            pl.BlockSpec((1, RET_V_DIM), const2),
        ],
        out_specs=pl.BlockSpec((ts, RET_V_DIM), lambda b, s: (row(b, s), 0)),
        scratch_shapes=[pltpu.VMEM((RET_HEADS, QK_HEAD, V_HEAD), F32)],
        compiler_params=_params(("parallel", "arbitrary")),
        name="retention",
    )(positions.reshape(M, 1), proj, proj, proj, proj, invf, sgn, mask, xi, zeta, cd,
      gn_g.reshape(1, RET_V_DIM), gn_b.reshape(1, RET_V_DIM))


def _mixout_kernel(ac_ref, ar_ref, gc_ref, gr_ref, x_ref, gm_ref, wc_ref, wr_ref, wo_ref, o_ref):
    yc = jnp.dot(ac_ref[...], wc_ref[...], preferred_element_type=F32)
    yr = jnp.dot(ar_ref[...], wr_ref[...], preferred_element_type=F32)
    merged = (jax.nn.sigmoid(gc_ref[...].astype(F32)) * yc
              + jax.nn.sigmoid(gr_ref[...].astype(F32)) * yr)
    out = jnp.dot(merged.astype(BF16), wo_ref[...], preferred_element_type=F32)
    o_ref[...] = x_ref[...] + gm_ref[0] * out


def _mixout(a_conv, a_ret, proj, x2d, gate_m, wc, wr, wo, seq):
    M, D = x2d.shape
    tm = 256
    per_b = seq // tm
    resident = functools.partial(pl.BlockSpec, pipeline_mode=pl.Buffered(1))
    return pl.pallas_call(
        _mixout_kernel,
        out_shape=jax.ShapeDtypeStruct((M, D), F32),
        grid=(M // tm,),
        in_specs=[
            pl.BlockSpec((tm, CONV_DIM), lambda i: (i, 0)),
            pl.BlockSpec((tm, RET_V_DIM), lambda i: (i, 0)),
            pl.BlockSpec((tm, D), lambda i: (i, 4)),
            pl.BlockSpec((tm, D), lambda i: (i, 5)),
            pl.BlockSpec((tm, D), lambda i: (i, 0)),
            pl.BlockSpec((1, 1, D), lambda i: (i // per_b, 0, 0)),
            resident((CONV_DIM, D), lambda i: (0, 0)),
            resident((RET_V_DIM, D), lambda i: (0, 0)),
            resident((D, D), lambda i: (0, 0)),
        ],
        out_specs=pl.BlockSpec((tm, D), lambda i: (i, 0)),
        compiler_params=_params(("parallel",)),
        name="mix_out",
    )(a_conv, a_ret, proj, proj, x2d, gate_m, wc, wr, wo)


def _ffn_kernel(x_ref, g_ref, sh_ref, sc_ref, gf_ref, w1_ref, w2_ref, gfin_ref, o_ref, h_ref, acc_ref):
    j = pl.program_id(1)

    @pl.when(j == 0)
    def _():
        h_ref[...] = _rms_modulate(x_ref[...], g_ref[...], sh_ref[0], sc_ref[0]).astype(BF16)
        acc_ref[...] = jnp.zeros(acc_ref.shape, F32)

    hid = jnp.dot(h_ref[...], w1_ref[...], preferred_element_type=F32)
    hid = jnp.square(jnp.maximum(hid, 0.0)).astype(BF16)
    acc_ref[...] += jnp.dot(hid, w2_ref[...], preferred_element_type=F32)

    @pl.when(j == pl.num_programs(1) - 1)
    def _():
        x2 = x_ref[...] + gf_ref[0] * acc_ref[...]
        ms = jnp.mean(x2 * x2, axis=-1, keepdims=True)
        o_ref[...] = x2 * lax.rsqrt(ms + EPS) * gfin_ref[...]


def _ffn(x1, g, shift, scale, gate, w1, w2, g_final, seq):
    M, D = x1.shape
    F = w1.shape[1]
    tm, tf = 512, 1024
    per_b = seq // tm
    bidx = lambda i, j: (i // per_b, 0, 0)
    return pl.pallas_call(
        _ffn_kernel,
        out_shape=jax.ShapeDtypeStruct((M, D), F32),
        grid=(M // tm, F // tf),
        in_specs=[
            pl.BlockSpec((tm, D), lambda i, j: (i, 0)),
            pl.BlockSpec((1, D), lambda i, j: (0, 0)),
            pl.BlockSpec((1, 1, D), bidx),
            pl.BlockSpec((1, 1, D), bidx),
            pl.BlockSpec((1, 1, D), bidx),
            pl.BlockSpec((D, tf), lambda i, j: (0, j)),
            pl.BlockSpec((tf, D), lambda i, j: (j, 0)),
            pl.BlockSpec((1, D), lambda i, j: (0, 0)),
        ],
        out_specs=pl.BlockSpec((tm, D), lambda i, j: (i, 0)),
        scratch_shapes=[pltpu.VMEM((tm, D), BF16), pltpu.VMEM((tm, D), F32)],
        compiler_params=_params(("parallel", "arbitrary")),
        name="ffn",
    )(x1, g, shift, scale, gate, w1, w2, g_final)


def kernel(x, c, positions, w_ada, b_ada, g_norm_mix, w_in, conv_w, conv_b, conv_ln_g, conv_ln_b,
           w_conv_out, ret_gn_g, ret_gn_b, w_ret_out, w_out, g_norm_ffn, w_ff1, w_ff2, g_norm_final):
    B, S, D = x.shape
    assert w_ada.shape[0] == 1, "kernel supports DEPTH == 1"
    l = 0
    x2d = x.reshape(B * S, D)
    mod = _mod(c, w_ada[l], b_ada[l]).reshape(B, N_MOD, 1, D)
    shift_m, scale_m, gate_m, shift_f, scale_f, gate_f = (mod[:, i] for i in range(N_MOD))
    proj = _inproj(x2d, g_norm_mix[l].reshape(1, D), shift_m, scale_m, w_in[l].astype(BF16), S)
    a_conv = _conv(proj, conv_w[l], conv_b[l], conv_ln_g[l], conv_ln_b[l], B, S)
    a_ret = _ret(proj, positions, ret_gn_g[l], ret_gn_b[l], B, S)
    x1 = _mixout(a_conv, a_ret, proj, x2d, gate_m, w_conv_out[l].astype(BF16),
                 w_ret_out[l].astype(BF16), w_out[l].astype(BF16), S)
    out = _ffn(x1, g_norm_ffn[l].reshape(1, D), shift_f, scale_f, gate_f,
               w_ff1[l].astype(BF16), w_ff2[l].astype(BF16), g_norm_final.reshape(1, D), S)
    return out.reshape(B, S, D)
```

```python
import functools

import jax
import jax.numpy as jnp
from jax import lax
from jax.experimental import pallas as pl
from jax.experimental.pallas import tpu as pltpu

D_MODEL = 2048
CONV_DIM = D_MODEL // 2
CONV_WIDTH = 31
RET_HEADS = 8
RET_QK_DIM = D_MODEL // 2
RET_V_DIM = D_MODEL
QK_HEAD = RET_QK_DIM // RET_HEADS
V_HEAD = RET_V_DIM // RET_HEADS
D_FF = 4 * D_MODEL
ROPE_BASE = 10000.0
EPS = 1e-6
IN_COLS = 2 * CONV_DIM + 2 * RET_QK_DIM + 2 * RET_V_DIM + 2 * D_MODEL
N_MOD = 6

RET_CHUNK = 256
RET_ROWS = 128
CONV_HIST = 32
CONV_ROWS = 128
CONV_LANES = 128

VMEM_LIMIT = 56 * 1024 * 1024

BF16 = jnp.bfloat16
F32 = jnp.float32


def _params(sem, vmem=VMEM_LIMIT):
    return pltpu.CompilerParams(dimension_semantics=sem, vmem_limit_bytes=vmem)


def _mod_kernel(c_ref, w_ref, b_ref, o_ref):
    c = c_ref[...]
    ca = (c * jax.nn.sigmoid(c)).astype(BF16)
    o_ref[...] = jnp.dot(ca, w_ref[...].astype(BF16), preferred_element_type=F32) + b_ref[...]


def _mod(c, w_ada, b_ada):
    B, D = c.shape
    N = w_ada.shape[1]
    tn = 1024
    return pl.pallas_call(
        _mod_kernel,
        out_shape=jax.ShapeDtypeStruct((B, N), F32),
        grid=(N // tn,),
        in_specs=[
            pl.BlockSpec((B, D), lambda j: (0, 0)),
            pl.BlockSpec((D, tn), lambda j: (0, j)),
            pl.BlockSpec((1, tn), lambda j: (0, j)),
        ],
        out_specs=pl.BlockSpec((B, tn), lambda j: (0, j)),
        compiler_params=_params(("arbitrary",)),
        name="mod",
    )(c, w_ada, b_ada.reshape(1, N))


NORM_ROWS = 16
NORM_UNROLL = 8


def _row_loop(n_rows, fn):
    def body(r, carry):
        fn(pl.multiple_of(r * NORM_ROWS, NORM_ROWS))
        return carry

    lax.fori_loop(0, n_rows // NORM_ROWS, body, 0, unroll=NORM_UNROLL)


def _rms_modulate_into(x_ref, h_ref, g, shift, scale):
    gs = g * (1.0 + scale)

    def rows(r0):
        x = x_ref[pl.ds(r0, NORM_ROWS), :]
        ms = jnp.mean(x * x, axis=-1, keepdims=True)
        h_ref[pl.ds(r0, NORM_ROWS), :] = (x * lax.rsqrt(ms + EPS) * gs + shift).astype(h_ref.dtype)

    _row_loop(x_ref.shape[0], rows)


def _inproj_kernel(x_ref, g_ref, sh_ref, sc_ref, w_ref, o_ref, h_ref):
    @pl.when(pl.program_id(1) == 0)
    def _():
        _rms_modulate_into(x_ref, h_ref, g_ref[...], sh_ref[0], sc_ref[0])

    o_ref[...] = jnp.dot(h_ref[...], w_ref[...], preferred_element_type=F32).astype(o_ref.dtype)


def _inproj(x2d, g, shift, scale, w, seq):
    M, D = x2d.shape
    N = w.shape[1]
    tm, tn = 1024, 1024
    per_b = seq // tm
    return pl.pallas_call(
        _inproj_kernel,
        out_shape=jax.ShapeDtypeStruct((M, N), BF16),
        grid=(M // tm, N // tn),
        in_specs=[
            pl.BlockSpec((tm, D), lambda i, j: (i, 0)),
            pl.BlockSpec((1, D), lambda i, j: (0, 0)),
            pl.BlockSpec((1, 1, D), lambda i, j: (i // per_b, 0, 0)),
            pl.BlockSpec((1, 1, D), lambda i, j: (i // per_b, 0, 0)),
            pl.BlockSpec((D, tn), lambda i, j: (0, j)),
        ],
        out_specs=pl.BlockSpec((tm, tn), lambda i, j: (i, j)),
        scratch_shapes=[pltpu.VMEM((tm, D), BF16)],
        compiler_params=_params(("parallel", "arbitrary")),
        name="in_proj",
    )(x2d, g, shift, scale, w)


def _conv_kernel(a_ref, gt_ref, w_ref, b_ref, lg_ref, lb_ref, o_ref, vext_ref, y_ref, *, ts, rows):
    s = pl.program_id(1)

    @pl.when(s == 0)
    def _():
        vext_ref[0:CONV_HIST, :] = jnp.zeros((CONV_HIST, CONV_DIM), F32)

    @pl.when(s != 0)
    def _():
        vext_ref[0:CONV_HIST, :] = vext_ref[ts:ts + CONV_HIST, :]

    SUB = 8
    R, LC = CONV_ROWS, CONV_LANES
    lg = lg_ref[...]
    lb = lb_ref[...]
    for r0 in range(0, ts, R):
        a = a_ref[r0:r0 + R, :].astype(F32)
        gt = gt_ref[r0:r0 + R, :].astype(F32)
        vext_ref[CONV_HIST + r0:CONV_HIST + r0 + R, :] = a * jax.nn.sigmoid(gt)
        for c in range(CONV_DIM // LC):
            ls = slice(c * LC, (c + 1) * LC)
            acc = None
            for b in range(SUB):
                p = None
                for a8 in range(-(-CONV_WIDTH // SUB)):
                    j = SUB * a8 + b
                    if j >= CONV_WIDTH:
                        continue
                    start = r0 + CONV_HIST - SUB - SUB * a8
                    k = CONV_WIDTH - 1 - j
                    term = vext_ref[start:start + R + SUB, ls] * w_ref[k:k + 1, ls]
                    p = term if p is None else p + term
                if b:
                    p = pltpu.roll(p, b, 0)
                p = p[SUB:]
                acc = p if acc is None else acc + p
            y_ref[r0:r0 + R, ls] = acc + b_ref[:, ls]
        for q0 in range(r0, r0 + R, rows):
            y = y_ref[q0:q0 + rows, :]
            mu = jnp.mean(y, axis=-1, keepdims=True)
            yc = y - mu
            var = jnp.mean(yc * yc, axis=-1, keepdims=True)
            yn = yc * lax.rsqrt(var + EPS) * lg + lb
            o_ref[q0:q0 + rows, :] = (yn * jax.nn.sigmoid(yn)).astype(o_ref.dtype)


def _conv(proj, conv_w, conv_b, ln_g, ln_b, batch, seq):
    M = proj.shape[0]
    ts, rows = 512, 16
    per_b = seq // ts
    C = CONV_DIM
    return pl.pallas_call(
        functools.partial(_conv_kernel, ts=ts, rows=rows),
        out_shape=jax.ShapeDtypeStruct((M, C), BF16),
        grid=(batch, per_b),
        in_specs=[
            pl.BlockSpec((ts, C), lambda b, s: (b * per_b + s, 0)),
            pl.BlockSpec((ts, C), lambda b, s: (b * per_b + s, 1)),
            pl.BlockSpec((CONV_WIDTH, C), lambda b, s: (0, 0)),
            pl.BlockSpec((1, C), lambda b, s: (0, 0)),
            pl.BlockSpec((1, C), lambda b, s: (0, 0)),
            pl.BlockSpec((1, C), lambda b, s: (0, 0)),
        ],
        out_specs=pl.BlockSpec((ts, C), lambda b, s: (b * per_b + s, 0)),
        scratch_shapes=[pltpu.VMEM((CONV_HIST + ts, C), F32), pltpu.VMEM((ts, C), F32)],
        compiler_params=_params(("parallel", "arbitrary")),
        name="conv_branch",
    )(proj, proj, conv_w, conv_b.reshape(1, C), ln_g.reshape(1, C), ln_b.reshape(1, C))


def _ret_kernel(pos_ref, q_ref, k_ref, v_ref, g_ref, invf_ref, sgn_ref, dq_ref, dk_ref,
                cd_ref, gng_ref, gnb_ref, o_ref, r_ref, *, ts):
    s = pl.program_id(1)

    @pl.when(s == 0)
    def _():
        r_ref[...] = jnp.zeros(r_ref.shape, F32)

    ang = pos_ref[...].astype(F32) * invf_ref[...]
    cosf = jnp.cos(ang)
    sinf = jnp.sin(ang) * sgn_ref[...]
    C, SB = RET_CHUNK, RET_ROWS
    causal = []
    for m in range(C // SB):
        row = lax.broadcasted_iota(jnp.int32, (SB, (m + 1) * SB), 0) + m * SB
        col = lax.broadcasted_iota(jnp.int32, (SB, (m + 1) * SB), 1)
        causal.append(col <= row)
    for n in range(ts // C):
        rs = slice(n * C, (n + 1) * C)
        cs_, sn_ = cosf[rs], sinf[rs]
        qds, kds, ps = [], [], []
        for h in range(RET_HEADS):
            qs = slice(h * QK_HEAD, (h + 1) * QK_HEAD)
            q = q_ref[rs, qs].astype(F32)
            k = k_ref[rs, qs].astype(F32)
            qd = ((q * cs_ + pltpu.roll(q, QK_HEAD // 2, 1) * sn_) * dq_ref[h]).astype(BF16)
            kd = ((k * cs_ + pltpu.roll(k, QK_HEAD // 2, 1) * sn_) * dk_ref[h]).astype(BF16)
            qds.append(qd)
            kds.append(kd)
            for m in range(C // SB):
                kcols = (m + 1) * SB
                sc = lax.dot_general(qd[m * SB:(m + 1) * SB], kd[:kcols], (((1,), (1,)), ((), ())),
                                     preferred_element_type=F32)
                ps.append(jnp.where(causal[m], sc, 0.0).astype(BF16))
        for h in range(RET_HEADS):
            vs = slice(h * V_HEAD, (h + 1) * V_HEAD)
            v = v_ref[rs, vs]
            Rb = r_ref[h].astype(BF16)
            gng = gng_ref[:, vs]
            gnb = gnb_ref[:, vs]
            for m in range(C // SB):
                kcols = (m + 1) * SB
                lhs = jnp.concatenate([ps[h * (C // SB) + m], qds[h][m * SB:(m + 1) * SB]], axis=1)
                rhs = jnp.concatenate([v[:kcols], Rb], axis=0)
                o = jnp.dot(lhs, rhs, preferred_element_type=F32)
                mu = jnp.mean(o, axis=-1, keepdims=True)
                oc = o - mu
                var = jnp.mean(oc * oc, axis=-1, keepdims=True)
                on = oc * lax.rsqrt(var + EPS) * gng + gnb
                orow = slice(n * C + m * SB, n * C + (m + 1) * SB)
                g = g_ref[orow, vs].astype(F32)
                o_ref[orow, vs] = (g * jax.nn.sigmoid(g) * on).astype(o_ref.dtype)
        for h in range(RET_HEADS):
            vs = slice(h * V_HEAD, (h + 1) * V_HEAD)
            kv = lax.dot_general(kds[h], v_ref[rs, vs], (((0,), (0,)), ((), ())),
                                 preferred_element_type=F32)
            r_ref[h] = (r_ref[h] + kv) * cd_ref[h]


def _ret_tables():
    C = RET_CHUNK
    log_gamma = jnp.log(1.0 - jnp.exp2(-5.0 - jnp.arange(RET_HEADS, dtype=F32)))
    idx = jnp.arange(C, dtype=F32)
    dq = jnp.exp((idx + 1.0)[None, :] * log_gamma[:, None])
    dk = jnp.exp(-(idx + 1.0)[None, :] * log_gamma[:, None]) * (QK_HEAD ** -0.5)
    cd = jnp.exp(C * log_gamma)
    half = QK_HEAD // 2
    inv_freq = ROPE_BASE ** (-jnp.arange(0, half, dtype=F32) / half)
    invf = jnp.concatenate([inv_freq, inv_freq]).reshape(1, QK_HEAD)
    sgn = jnp.concatenate([-jnp.ones((half,), F32), jnp.ones((half,), F32)]).reshape(1, QK_HEAD)
    cd_b = jnp.broadcast_to(cd[:, None, None], (RET_HEADS, 1, V_HEAD))
    return invf, sgn, dq[:, :, None], dk[:, :, None], cd_b


def _ret(proj, positions, gn_g, gn_b, batch, seq):
    M = proj.shape[0]
    ts = 512
    per_b = seq // ts
    C = RET_CHUNK
    invf, sgn, dq, dk, cd = _ret_tables()
    row = lambda b, s: b * per_b + s
    const2 = lambda b, s: (0, 0)
    const3 = lambda b, s: (0, 0, 0)
    return pl.pallas_call(
        functools.partial(_ret_kernel, ts=ts),
        out_shape=jax.ShapeDtypeStruct((M, RET_V_DIM), BF16),
        grid=(batch, per_b),
        in_specs=[
            pl.BlockSpec((ts, 1), lambda b, s: (row(b, s), 0)),
            pl.BlockSpec((ts, RET_QK_DIM), lambda b, s: (row(b, s), 2)),
            pl.BlockSpec((ts, RET_QK_DIM), lambda b, s: (row(b, s), 3)),
            pl.BlockSpec((ts, RET_V_DIM), lambda b, s: (row(b, s), 2)),
            pl.BlockSpec((ts, RET_V_DIM), lambda b, s: (row(b, s), 3)),
            pl.BlockSpec((1, QK_HEAD), const2),
            pl.BlockSpec((1, QK_HEAD), const2),
            pl.BlockSpec((RET_HEADS, C, 1), const3),
            pl.BlockSpec((RET_HEADS, C, 1), const3),
            pl.BlockSpec((RET_HEADS, 1, V_HEAD), const3),
            pl.BlockSpec((1, RET_V_DIM), const2),
            pl.BlockSpec((1, RET_V_DIM), const2),
        ],
        out_specs=pl.BlockSpec((ts, RET_V_DIM), lambda b, s: (row(b, s), 0)),
        scratch_shapes=[pltpu.VMEM((RET_HEADS, QK_HEAD, V_HEAD), F32)],
        compiler_params=_params(("parallel", "arbitrary")),
        name="retention",
    )(positions.reshape(M, 1), proj, proj, proj, proj, invf, sgn, dq, dk, cd,
      gn_g.reshape(1, RET_V_DIM), gn_b.reshape(1, RET_V_DIM))


def _mixout_kernel(ac_ref, ar_ref, gc_ref, gr_ref, x_ref, gm_ref, wc_ref, wr_ref, wo_ref, o_ref):
    yc = jnp.dot(ac_ref[...], wc_ref[...], preferred_element_type=F32)
    yr = jnp.dot(ar_ref[...], wr_ref[...], preferred_element_type=F32)
    merged = (jax.nn.sigmoid(gc_ref[...].astype(F32)) * yc
              + jax.nn.sigmoid(gr_ref[...].astype(F32)) * yr)
    out = jnp.dot(merged.astype(BF16), wo_ref[...], preferred_element_type=F32)
    o_ref[...] = x_ref[...] + gm_ref[0] * out


def _mixout(a_conv, a_ret, proj, x2d, gate_m, wc, wr, wo, seq):
    M, D = x2d.shape
    tm = 256
    per_b = seq // tm
    resident = functools.partial(pl.BlockSpec, pipeline_mode=pl.Buffered(1))
    return pl.pallas_call(
        _mixout_kernel,
        out_shape=jax.ShapeDtypeStruct((M, D), F32),
        grid=(M // tm,),
        in_specs=[
            pl.BlockSpec((tm, CONV_DIM), lambda i: (i, 0)),
            pl.BlockSpec((tm, RET_V_DIM), lambda i: (i, 0)),
            pl.BlockSpec((tm, D), lambda i: (i, 4)),
            pl.BlockSpec((tm, D), lambda i: (i, 5)),
            pl.BlockSpec((tm, D), lambda i: (i, 0)),
            pl.BlockSpec((1, 1, D), lambda i: (i // per_b, 0, 0)),
            resident((CONV_DIM, D), lambda i: (0, 0)),
            resident((RET_V_DIM, D), lambda i: (0, 0)),
            resident((D, D), lambda i: (0, 0)),
        ],
        out_specs=pl.BlockSpec((tm, D), lambda i: (i, 0)),
        compiler_params=_params(("parallel",)),
        name="mix_out",
    )(a_conv, a_ret, proj, proj, x2d, gate_m, wc, wr, wo)


def _ffn_kernel(x_ref, g_ref, sh_ref, sc_ref, gf_ref, w1_ref, w2_ref, gfin_ref, o_ref, h_ref, acc_ref):
    j = pl.program_id(1)

    @pl.when(j == 0)
    def _():
        _rms_modulate_into(x_ref, h_ref, g_ref[...], sh_ref[0], sc_ref[0])
        acc_ref[...] = jnp.zeros(acc_ref.shape, F32)

    hid = jnp.dot(h_ref[...], w1_ref[...], preferred_element_type=F32)
    hid = jnp.square(jnp.maximum(hid, 0.0)).astype(BF16)
    acc_ref[...] += jnp.dot(hid, w2_ref[...], preferred_element_type=F32)

    @pl.when(j == pl.num_programs(1) - 1)
    def _():
        gf = gf_ref[0]
        gfin = gfin_ref[...]

        def rows(r0):
            rs = pl.ds(r0, NORM_ROWS)
            x2 = x_ref[rs, :] + gf * acc_ref[rs, :]
            ms = jnp.mean(x2 * x2, axis=-1, keepdims=True)
            o_ref[rs, :] = x2 * lax.rsqrt(ms + EPS) * gfin

        _row_loop(x_ref.shape[0], rows)


def _ffn(x1, g, shift, scale, gate, w1, w2, g_final, seq):
    M, D = x1.shape
    F = w1.shape[1]
    tm, tf = 512, 1024
    per_b = seq // tm
    bidx = lambda i, j: (i // per_b, 0, 0)
    return pl.pallas_call(
        _ffn_kernel,
        out_shape=jax.ShapeDtypeStruct((M, D), F32),
        grid=(M // tm, F // tf),
        in_specs=[
            pl.BlockSpec((tm, D), lambda i, j: (i, 0)),
            pl.BlockSpec((1, D), lambda i, j: (0, 0)),
            pl.BlockSpec((1, 1, D), bidx),
            pl.BlockSpec((1, 1, D), bidx),
            pl.BlockSpec((1, 1, D), bidx),
            pl.BlockSpec((D, tf), lambda i, j: (0, j)),
            pl.BlockSpec((tf, D), lambda i, j: (j, 0)),
            pl.BlockSpec((1, D), lambda i, j: (0, 0)),
        ],
        out_specs=pl.BlockSpec((tm, D), lambda i, j: (i, 0)),
        scratch_shapes=[pltpu.VMEM((tm, D), BF16), pltpu.VMEM((tm, D), F32)],
        compiler_params=_params(("parallel", "arbitrary")),
        name="ffn",
    )(x1, g, shift, scale, gate, w1, w2, g_final)


def kernel(x, c, positions, w_ada, b_ada, g_norm_mix, w_in, conv_w, conv_b, conv_ln_g, conv_ln_b,
           w_conv_out, ret_gn_g, ret_gn_b, w_ret_out, w_out, g_norm_ffn, w_ff1, w_ff2, g_norm_final):
    B, S, D = x.shape
    assert w_ada.shape[0] == 1, "kernel supports DEPTH == 1"
    l = 0
    x2d = x.reshape(B * S, D)
    mod = _mod(c, w_ada[l], b_ada[l]).reshape(B, N_MOD, 1, D)
    shift_m, scale_m, gate_m, shift_f, scale_f, gate_f = (mod[:, i] for i in range(N_MOD))
    proj = _inproj(x2d, g_norm_mix[l].reshape(1, D), shift_m, scale_m, w_in[l].astype(BF16), S)
    a_conv = _conv(proj, conv_w[l], conv_b[l], conv_ln_g[l], conv_ln_b[l], B, S)
    a_ret = _ret(proj, positions, ret_gn_g[l], ret_gn_b[l], B, S)
    x1 = _mixout(a_conv, a_ret, proj, x2d, gate_m, w_conv_out[l].astype(BF16),
                 w_ret_out[l].astype(BF16), w_out[l].astype(BF16), S)
    out = _ffn(x1, g_norm_ffn[l].reshape(1, D), shift_f, scale_f, gate_f,
               w_ff1[l].astype(BF16), w_ff2[l].astype(BF16), g_norm_final.reshape(1, D), S)
    return out.reshape(B, S, D)
```

```python
import functools

import jax
import jax.numpy as jnp
from jax import lax
from jax.experimental import pallas as pl
from jax.experimental.pallas import tpu as pltpu

D_MODEL = 2048
CONV_DIM = D_MODEL // 2
CONV_WIDTH = 31
RET_HEADS = 8
RET_QK_DIM = D_MODEL // 2
RET_V_DIM = D_MODEL
QK_HEAD = RET_QK_DIM // RET_HEADS
V_HEAD = RET_V_DIM // RET_HEADS
D_FF = 4 * D_MODEL
ROPE_BASE = 10000.0
EPS = 1e-6
IN_COLS = 2 * CONV_DIM + 2 * RET_QK_DIM + 2 * RET_V_DIM + 2 * D_MODEL
N_MOD = 6

RET_CHUNK = 256
RET_ROWS = 128
CONV_HIST = 32
CONV_ROWS = 128
CONV_PIECE_ROWS = 32
CONV_LANES = 128

VMEM_LIMIT = 56 * 1024 * 1024

BF16 = jnp.bfloat16
F32 = jnp.float32


def _params(sem, vmem=VMEM_LIMIT):
    return pltpu.CompilerParams(dimension_semantics=sem, vmem_limit_bytes=vmem)


def _mod_kernel(c_ref, w_ref, b_ref, o_ref):
    c = c_ref[...]
    ca = (c * jax.nn.sigmoid(c)).astype(BF16)
    o_ref[...] = jnp.dot(ca, w_ref[...].astype(BF16), preferred_element_type=F32) + b_ref[...]


def _mod(c, w_ada, b_ada):
    B, D = c.shape
    N = w_ada.shape[1]
    tn = 1024
    return pl.pallas_call(
        _mod_kernel,
        out_shape=jax.ShapeDtypeStruct((B, N), F32),
        grid=(N // tn,),
        in_specs=[
            pl.BlockSpec((B, D), lambda j: (0, 0)),
            pl.BlockSpec((D, tn), lambda j: (0, j)),
            pl.BlockSpec((1, tn), lambda j: (0, j)),
        ],
        out_specs=pl.BlockSpec((B, tn), lambda j: (0, j)),
        compiler_params=_params(("arbitrary",)),
        name="mod",
    )(c, w_ada, b_ada.reshape(1, N))


NORM_ROWS = 16
NORM_UNROLL = 8


def _row_loop(n_rows, fn):
    def body(r, carry):
        fn(pl.multiple_of(r * NORM_ROWS, NORM_ROWS))
        return carry

    lax.fori_loop(0, n_rows // NORM_ROWS, body, 0, unroll=NORM_UNROLL)


def _rms_modulate_into(x_ref, h_ref, g, shift, scale):
    gs = g * (1.0 + scale)

    def rows(r0):
        x = x_ref[pl.ds(r0, NORM_ROWS), :]
        ms = jnp.mean(x * x, axis=-1, keepdims=True)
        h_ref[pl.ds(r0, NORM_ROWS), :] = (x * lax.rsqrt(ms + EPS) * gs + shift).astype(h_ref.dtype)

    _row_loop(x_ref.shape[0], rows)


def _conv_rows(win_ref, y_ref, o_ref, w_ref, b_ref, lg, lb, zero_ref):
    SUB = 8
    R, LC = CONV_PIECE_ROWS, CONV_LANES
    zero = zero_ref[...]

    def token(v):
        return lax.bitcast_convert_type(pltpu.roll(v, 1, 1), jnp.int32) & zero

    def after(x, tok):
        if tok is None:
            return x
        return lax.bitcast_convert_type(lax.bitcast_convert_type(x, jnp.int32) | tok[0:x.shape[0]], F32)

    one_bits = jnp.int32(0x3F800000)
    tok = None
    for q in range(0, CONV_ROWS, NORM_ROWS):
        y = y_ref[q:q + NORM_ROWS, :]
        if tok is not None:
            one = lax.bitcast_convert_type(tok[0:1] | one_bits, F32)
            y = jnp.concatenate([y[:, 0:LC] * one, y[:, LC:]], axis=1)
        mu = jnp.mean(y, axis=-1, keepdims=True)
        yc = y - mu
        var = jnp.mean(yc * yc, axis=-1, keepdims=True)
        yn = yc * lax.rsqrt(var + EPS) * lg + lb
        out = yn * jax.nn.sigmoid(yn)
        tok = token(out[0:8, 0:LC])
        o_ref[q:q + NORM_ROWS, :] = out.astype(o_ref.dtype)
    tok = None
    for c in range(CONV_DIM // LC):
        ls = slice(c * LC, (c + 1) * LC)
        for r0 in range(0, CONV_ROWS, R):
            acc = None
            for b in range(SUB):
                p = None
                for a8 in range(-(-CONV_WIDTH // SUB)):
                    j = SUB * a8 + b
                    if j >= CONV_WIDTH:
                        continue
                    start = r0 + CONV_HIST - SUB - SUB * a8
                    k = CONV_WIDTH - 1 - j
                    w = w_ref[k:k + 1, ls]
                    if p is None:
                        w = after(w, tok)
                    term = win_ref[start:start + R + SUB, ls] * w
                    p = term if p is None else p + term
                if b:
                    p = pltpu.roll(p, b, 0)
                p = p[SUB:]
                acc = p if acc is None else acc + p
            tok = token(acc[0:8])
            y_ref[r0:r0 + R, ls] = acc + b_ref[:, ls]


GLU_STEP = 2
CONV_FIRST_STEP = 3
NORM_FIRST_STEP = 4


def _block_of_step(j, first_step, n_blocks):
    return jnp.clip(j - first_step, 0, n_blocks - 1)


def _inproj_kernel(x_ref, g_ref, sh_ref, sc_ref, w_ref, cw_ref, cb_ref, lg_ref, lb_ref, zero_ref,
                   o_ref, ac_ref, h_ref, stash_ref, vext_ref, win_ref, y_ref, *, tiles_per_seq):
    i = pl.program_id(0)
    j = pl.program_id(1)
    tm = x_ref.shape[0]

    @pl.when(j == 0)
    def _():
        _rms_modulate_into(x_ref, h_ref, g_ref[...], sh_ref[0], sc_ref[0])

    @pl.when(jnp.logical_and(i == 0, j == 0))
    def _():
        vext_ref[...] = jnp.zeros(vext_ref.shape, F32)
        win_ref[...] = jnp.zeros(win_ref.shape, F32)
        y_ref[...] = jnp.zeros(y_ref.shape, F32)

    @pl.when(j == GLU_STEP)
    def _():
        first = i % tiles_per_seq == 0

        @pl.when(first)
        def _():
            vext_ref[0:CONV_HIST, :] = jnp.zeros((CONV_HIST, CONV_DIM), F32)

        @pl.when(jnp.logical_not(first))
        def _():
            vext_ref[0:CONV_HIST, :] = vext_ref[tm:tm + CONV_HIST, :]

        a = stash_ref[0].astype(F32)
        gt = stash_ref[1].astype(F32)
        vext_ref[CONV_HIST:CONV_HIST + tm, :] = a * jax.nn.sigmoid(gt)

    _conv_rows(win_ref, y_ref, ac_ref, cw_ref, cb_ref, lg_ref[...], lb_ref[...], zero_ref)

    o_ref[...] = jnp.dot(h_ref[...], w_ref[...], preferred_element_type=F32).astype(o_ref.dtype)

    nxt = _block_of_step(j + 1, CONV_FIRST_STEP, tm // CONV_ROWS)
    win_ref[...] = vext_ref[pl.ds(pl.multiple_of(nxt * CONV_ROWS, CONV_ROWS), CONV_HIST + CONV_ROWS), :]

    @pl.when(j < 2)
    def _():
        stash_ref[j] = o_ref[...]


def _inproj(x2d, g, shift, scale, w, conv_w, conv_b, ln_g, ln_b, seq):
    M, D = x2d.shape
    N = w.shape[1]
    C = CONV_DIM
    tm, tn = 1024, 1024
    per_b = seq // tm
    n_blocks = tm // CONV_ROWS
    assert tn == C and N // tn >= NORM_FIRST_STEP + n_blocks
    const = lambda i, j: (0, 0)
    return pl.pallas_call(
        functools.partial(_inproj_kernel, tiles_per_seq=per_b),
        out_shape=(jax.ShapeDtypeStruct((M, N), BF16), jax.ShapeDtypeStruct((M, C), BF16)),
        grid=(M // tm, N // tn),
        in_specs=[
            pl.BlockSpec((tm, D), lambda i, j: (i, 0)),
            pl.BlockSpec((1, D), const),
            pl.BlockSpec((1, 1, D), lambda i, j: (i // per_b, 0, 0)),
            pl.BlockSpec((1, 1, D), lambda i, j: (i // per_b, 0, 0)),
            pl.BlockSpec((D, tn), lambda i, j: (0, j)),
            pl.BlockSpec((CONV_WIDTH, C), const),
            pl.BlockSpec((1, C), const),
            pl.BlockSpec((1, C), const),
            pl.BlockSpec((1, C), const),
            pl.BlockSpec((1, CONV_LANES), const),
        ],
        out_specs=(pl.BlockSpec((tm, tn), lambda i, j: (i, j)),
                   pl.BlockSpec((CONV_ROWS, C), lambda i, j: (i * n_blocks + _block_of_step(j, NORM_FIRST_STEP, n_blocks), 0))),
        scratch_shapes=[
            pltpu.VMEM((tm, D), BF16),
            pltpu.VMEM((2, tm, C), BF16),
            pltpu.VMEM((CONV_HIST + tm, C), F32),
            pltpu.VMEM((CONV_HIST + CONV_ROWS, C), F32),
            pltpu.VMEM((CONV_ROWS, C), F32),
        ],
        compiler_params=_params(("arbitrary", "arbitrary")),
        name="in_proj",
    )(x2d, g, shift, scale, w, conv_w, conv_b.reshape(1, C), ln_g.reshape(1, C), ln_b.reshape(1, C),
      jnp.zeros((1, CONV_LANES), jnp.int32))


def _ret_kernel(pos_ref, q_ref, k_ref, v_ref, g_ref, invf_ref, sgn_ref, dq_ref, dk_ref,
                cd_ref, gng_ref, gnb_ref, o_ref, r_ref, *, ts):
    s = pl.program_id(1)

    @pl.when(s == 0)
    def _():
        r_ref[...] = jnp.zeros(r_ref.shape, F32)

    ang = pos_ref[...].astype(F32) * invf_ref[...]
    cosf = jnp.cos(ang)
    sinf = jnp.sin(ang) * sgn_ref[...]
    C, SB = RET_CHUNK, RET_ROWS
    causal = []
    for m in range(C // SB):
        row = lax.broadcasted_iota(jnp.int32, (SB, (m + 1) * SB), 0) + m * SB
        col = lax.broadcasted_iota(jnp.int32, (SB, (m + 1) * SB), 1)
        causal.append(col <= row)
    for n in range(ts // C):
        rs = slice(n * C, (n + 1) * C)
        cs_, sn_ = cosf[rs], sinf[rs]
        qds, kds, ps = [], [], []
        for h in range(RET_HEADS):
            qs = slice(h * QK_HEAD, (h + 1) * QK_HEAD)
            q = q_ref[rs, qs].astype(F32)
            k = k_ref[rs, qs].astype(F32)
            qd = ((q * cs_ + pltpu.roll(q, QK_HEAD // 2, 1) * sn_) * dq_ref[h]).astype(BF16)
            kd = ((k * cs_ + pltpu.roll(k, QK_HEAD // 2, 1) * sn_) * dk_ref[h]).astype(BF16)
            qds.append(qd)
            kds.append(kd)
            for m in range(C // SB):
                kcols = (m + 1) * SB
                sc = lax.dot_general(qd[m * SB:(m + 1) * SB], kd[:kcols], (((1,), (1,)), ((), ())),
                                     preferred_element_type=F32)
                ps.append(jnp.where(causal[m], sc, 0.0).astype(BF16))
        for h in range(RET_HEADS):
            vs = slice(h * V_HEAD, (h + 1) * V_HEAD)
            v = v_ref[rs, vs]
            Rb = r_ref[h].astype(BF16)
            gng = gng_ref[:, vs]
            gnb = gnb_ref[:, vs]
            for m in range(C // SB):
                kcols = (m + 1) * SB
                lhs = jnp.concatenate([ps[h * (C // SB) + m], qds[h][m * SB:(m + 1) * SB]], axis=1)
                rhs = jnp.concatenate([v[:kcols], Rb], axis=0)
                o = jnp.dot(lhs, rhs, preferred_element_type=F32)
                mu = jnp.mean(o, axis=-1, keepdims=True)
                oc = o - mu
                var = jnp.mean(oc * oc, axis=-1, keepdims=True)
                on = oc * lax.rsqrt(var + EPS) * gng + gnb
                orow = slice(n * C + m * SB, n * C + (m + 1) * SB)
                g = g_ref[orow, vs].astype(F32)
                o_ref[orow, vs] = (g * jax.nn.sigmoid(g) * on).astype(o_ref.dtype)
        for h in range(RET_HEADS):
            vs = slice(h * V_HEAD, (h + 1) * V_HEAD)
            kv = lax.dot_general(kds[h], v_ref[rs, vs], (((0,), (0,)), ((), ())),
                                 preferred_element_type=F32)
            r_ref[h] = (r_ref[h] + kv) * cd_ref[h]


def _ret_tables():
    C = RET_CHUNK
    log_gamma = jnp.log(1.0 - jnp.exp2(-5.0 - jnp.arange(RET_HEADS, dtype=F32)))
    idx = jnp.arange(C, dtype=F32)
    dq = jnp.exp((idx + 1.0)[None, :] * log_gamma[:, None])
    dk = jnp.exp(-(idx + 1.0)[None, :] * log_gamma[:, None]) * (QK_HEAD ** -0.5)
    cd = jnp.exp(C * log_gamma)
    half = QK_HEAD // 2
    inv_freq = ROPE_BASE ** (-jnp.arange(0, half, dtype=F32) / half)
    invf = jnp.concatenate([inv_freq, inv_freq]).reshape(1, QK_HEAD)
    sgn = jnp.concatenate([-jnp.ones((half,), F32), jnp.ones((half,), F32)]).reshape(1, QK_HEAD)
    cd_b = jnp.broadcast_to(cd[:, None, None], (RET_HEADS, 1, V_HEAD))
    return invf, sgn, dq[:, :, None], dk[:, :, None], cd_b


def _ret(proj, positions, gn_g, gn_b, batch, seq):
    M = proj.shape[0]
    ts = 512
    per_b = seq // ts
    C = RET_CHUNK
    invf, sgn, dq, dk, cd = _ret_tables()
    row = lambda b, s: b * per_b + s
    const2 = lambda b, s: (0, 0)
    const3 = lambda b, s: (0, 0, 0)
    return pl.pallas_call(
        functools.partial(_ret_kernel, ts=ts),
        out_shape=jax.ShapeDtypeStruct((M, RET_V_DIM), BF16),
        grid=(batch, per_b),
        in_specs=[
            pl.BlockSpec((ts, 1), lambda b, s: (row(b, s), 0)),
            pl.BlockSpec((ts, RET_QK_DIM), lambda b, s: (row(b, s), 2)),
            pl.BlockSpec((ts, RET_QK_DIM), lambda b, s: (row(b, s), 3)),
            pl.BlockSpec((ts, RET_V_DIM), lambda b, s: (row(b, s), 2)),
            pl.BlockSpec((ts, RET_V_DIM), lambda b, s: (row(b, s), 3)),
            pl.BlockSpec((1, QK_HEAD), const2),
            pl.BlockSpec((1, QK_HEAD), const2),
            pl.BlockSpec((RET_HEADS, C, 1), const3),
            pl.BlockSpec((RET_HEADS, C, 1), const3),
            pl.BlockSpec((RET_HEADS, 1, V_HEAD), const3),
            pl.BlockSpec((1, RET_V_DIM), const2),
            pl.BlockSpec((1, RET_V_DIM), const2),
        ],
        out_specs=pl.BlockSpec((ts, RET_V_DIM), lambda b, s: (row(b, s), 0)),
        scratch_shapes=[pltpu.VMEM((RET_HEADS, QK_HEAD, V_HEAD), F32)],
        compiler_params=_params(("parallel", "arbitrary")),
        name="retention",
    )(positions.reshape(M, 1), proj, proj, proj, proj, invf, sgn, dq, dk, cd,
      gn_g.reshape(1, RET_V_DIM), gn_b.reshape(1, RET_V_DIM))


def _mixout_kernel(ac_ref, ar_ref, gc_ref, gr_ref, x_ref, gm_ref, wc_ref, wr_ref, wo_ref, o_ref):
    yc = jnp.dot(ac_ref[...], wc_ref[...], preferred_element_type=F32)
    yr = jnp.dot(ar_ref[...], wr_ref[...], preferred_element_type=F32)
    merged = (jax.nn.sigmoid(gc_ref[...].astype(F32)) * yc
              + jax.nn.sigmoid(gr_ref[...].astype(F32)) * yr)
    out = jnp.dot(merged.astype(BF16), wo_ref[...], preferred_element_type=F32)
    o_ref[...] = x_ref[...] + gm_ref[0] * out


def _mixout(a_conv, a_ret, proj, x2d, gate_m, wc, wr, wo, seq):
    M, D = x2d.shape
    tm = 256
    per_b = seq // tm
    resident = functools.partial(pl.BlockSpec, pipeline_mode=pl.Buffered(1))
    return pl.pallas_call(
        _mixout_kernel,
        out_shape=jax.ShapeDtypeStruct((M, D), F32),
        grid=(M // tm,),
        in_specs=[
            pl.BlockSpec((tm, CONV_DIM), lambda i: (i, 0)),
            pl.BlockSpec((tm, RET_V_DIM), lambda i: (i, 0)),
            pl.BlockSpec((tm, D), lambda i: (i, 4)),
            pl.BlockSpec((tm, D), lambda i: (i, 5)),
            pl.BlockSpec((tm, D), lambda i: (i, 0)),
            pl.BlockSpec((1, 1, D), lambda i: (i // per_b, 0, 0)),
            resident((CONV_DIM, D), lambda i: (0, 0)),
            resident((RET_V_DIM, D), lambda i: (0, 0)),
            resident((D, D), lambda i: (0, 0)),
        ],
        out_specs=pl.BlockSpec((tm, D), lambda i: (i, 0)),
        compiler_params=_params(("parallel",)),
        name="mix_out",
    )(a_conv, a_ret, proj, proj, x2d, gate_m, wc, wr, wo)


def _ffn_kernel(x_ref, g_ref, sh_ref, sc_ref, gf_ref, w1_ref, w2_ref, gfin_ref, o_ref, h_ref, acc_ref):
    j = pl.program_id(1)

    @pl.when(j == 0)
    def _():
        _rms_modulate_into(x_ref, h_ref, g_ref[...], sh_ref[0], sc_ref[0])
        acc_ref[...] = jnp.zeros(acc_ref.shape, F32)

    hid = jnp.dot(h_ref[...], w1_ref[...], preferred_element_type=F32)
    hid = jnp.square(jnp.maximum(hid, 0.0)).astype(BF16)
    acc_ref[...] += jnp.dot(hid, w2_ref[...], preferred_element_type=F32)

    @pl.when(j == pl.num_programs(1) - 1)
    def _():
        gf = gf_ref[0]
        gfin = gfin_ref[...]

        def rows(r0):
            rs = pl.ds(r0, NORM_ROWS)
            x2 = x_ref[rs, :] + gf * acc_ref[rs, :]
            ms = jnp.mean(x2 * x2, axis=-1, keepdims=True)
            o_ref[rs, :] = x2 * lax.rsqrt(ms + EPS) * gfin

        _row_loop(x_ref.shape[0], rows)


def _ffn(x1, g, shift, scale, gate, w1, w2, g_final, seq):
    M, D = x1.shape
    F = w1.shape[1]
    tm, tf = 512, 1024
    per_b = seq // tm
    bidx = lambda i, j: (i // per_b, 0, 0)
    return pl.pallas_call(
        _ffn_kernel,
        out_shape=jax.ShapeDtypeStruct((M, D), F32),
        grid=(M // tm, F // tf),
        in_specs=[
            pl.BlockSpec((tm, D), lambda i, j: (i, 0)),
            pl.BlockSpec((1, D), lambda i, j: (0, 0)),
            pl.BlockSpec((1, 1, D), bidx),
            pl.BlockSpec((1, 1, D), bidx),
            pl.BlockSpec((1, 1, D), bidx),
            pl.BlockSpec((D, tf), lambda i, j: (0, j)),
            pl.BlockSpec((tf, D), lambda i, j: (j, 0)),
            pl.BlockSpec((1, D), lambda i, j: (0, 0)),
        ],
        out_specs=pl.BlockSpec((tm, D), lambda i, j: (i, 0)),
        scratch_shapes=[pltpu.VMEM((tm, D), BF16), pltpu.VMEM((tm, D), F32)],
        compiler_params=_params(("parallel", "arbitrary")),
        name="ffn",
    )(x1, g, shift, scale, gate, w1, w2, g_final)


def kernel(x, c, positions, w_ada, b_ada, g_norm_mix, w_in, conv_w, conv_b, conv_ln_g, conv_ln_b,
           w_conv_out, ret_gn_g, ret_gn_b, w_ret_out, w_out, g_norm_ffn, w_ff1, w_ff2, g_norm_final):
    B, S, D = x.shape
    assert w_ada.shape[0] == 1, "kernel supports DEPTH == 1"
    l = 0
    x2d = x.reshape(B * S, D)
    mod = _mod(c, w_ada[l], b_ada[l]).reshape(B, N_MOD, 1, D)
    shift_m, scale_m, gate_m, shift_f, scale_f, gate_f = (mod[:, i] for i in range(N_MOD))
    proj, a_conv = _inproj(x2d, g_norm_mix[l].reshape(1, D), shift_m, scale_m, w_in[l].astype(BF16),
                           conv_w[l], conv_b[l], conv_ln_g[l], conv_ln_b[l], S)
    a_ret = _ret(proj, positions, ret_gn_g[l], ret_gn_b[l], B, S)
    x1 = _mixout(a_conv, a_ret, proj, x2d, gate_m, w_conv_out[l].astype(BF16),
                 w_ret_out[l].astype(BF16), w_out[l].astype(BF16), S)
    out = _ffn(x1, g_norm_ffn[l].reshape(1, D), shift_f, scale_f, gate_f,
               w_ff1[l].astype(BF16), w_ff2[l].astype(BF16), g_norm_final.reshape(1, D), S)
    return out.reshape(B, S, D)
```

```python
import functools

import jax
import jax.numpy as jnp
from jax import lax
from jax.experimental import pallas as pl
from jax.experimental.pallas import tpu as pltpu

D_MODEL = 2048
CONV_DIM = D_MODEL // 2
CONV_WIDTH = 31
RET_HEADS = 8
RET_QK_DIM = D_MODEL // 2
RET_V_DIM = D_MODEL
QK_HEAD = RET_QK_DIM // RET_HEADS
V_HEAD = RET_V_DIM // RET_HEADS
D_FF = 4 * D_MODEL
ROPE_BASE = 10000.0
EPS = 1e-6
IN_COLS = 2 * CONV_DIM + 2 * RET_QK_DIM + 2 * RET_V_DIM + 2 * D_MODEL
N_MOD = 6

RET_CHUNK = 256
RET_ROWS = 128
CONV_HIST = 32
CONV_ROWS = 128
CONV_PIECE_ROWS = 32
CONV_LANES = 128

VMEM_LIMIT = 56 * 1024 * 1024
FFN_VMEM_LIMIT = 60 * 1024 * 1024
BF16 = jnp.bfloat16
F32 = jnp.float32


def _params(sem, vmem=VMEM_LIMIT):
    return pltpu.CompilerParams(dimension_semantics=sem, vmem_limit_bytes=vmem)


def _mod_kernel(c_ref, w_ref, b_ref, o_ref):
    c = c_ref[...]
    ca = (c * jax.nn.sigmoid(c)).astype(BF16)
    o_ref[...] = jnp.dot(ca, w_ref[...].astype(BF16), preferred_element_type=F32) + b_ref[...]


def _mod(c, w_ada, b_ada):
    B, D = c.shape
    N = w_ada.shape[1]
    tn = 1024
    return pl.pallas_call(
        _mod_kernel,
        out_shape=jax.ShapeDtypeStruct((B, N), F32),
        grid=(N // tn,),
        in_specs=[
            pl.BlockSpec((B, D), lambda j: (0, 0)),
            pl.BlockSpec((D, tn), lambda j: (0, j)),
            pl.BlockSpec((1, tn), lambda j: (0, j)),
        ],
        out_specs=pl.BlockSpec((B, tn), lambda j: (0, j)),
        compiler_params=_params(("arbitrary",)),
        name="mod",
    )(c, w_ada, b_ada.reshape(1, N))


NORM_ROWS = 16
NORM_UNROLL = 8


def _row_loop(n_rows, fn):
    def body(r, carry):
        fn(pl.multiple_of(r * NORM_ROWS, NORM_ROWS))
        return carry

    lax.fori_loop(0, n_rows // NORM_ROWS, body, 0, unroll=NORM_UNROLL)


def _rms_modulate_into(x_ref, h_ref, g, shift, scale):
    gs = g * (1.0 + scale)

    def rows(r0):
        x = x_ref[pl.ds(r0, NORM_ROWS), :]
        ms = jnp.mean(x * x, axis=-1, keepdims=True)
        h_ref[pl.ds(r0, NORM_ROWS), :] = (x * lax.rsqrt(ms + EPS) * gs + shift).astype(h_ref.dtype)

    _row_loop(x_ref.shape[0], rows)


def _conv_rows(win_ref, y_ref, o_ref, w_ref, b_ref, lg, lb, zero_ref):
    SUB = 8
    R, LC = CONV_PIECE_ROWS, CONV_LANES
    zero = zero_ref[...]

    def token(v):
        return lax.bitcast_convert_type(pltpu.roll(v, 1, 1), jnp.int32) & zero

    def after(x, tok):
        if tok is None:
            return x
        return lax.bitcast_convert_type(lax.bitcast_convert_type(x, jnp.int32) | tok[0:x.shape[0]], F32)

    one_bits = jnp.int32(0x3F800000)
    tok = None
    for q in range(0, CONV_ROWS, NORM_ROWS):
        y = y_ref[q:q + NORM_ROWS, :]
        if tok is not None:
            one = lax.bitcast_convert_type(tok[0:1] | one_bits, F32)
            y = jnp.concatenate([y[:, 0:LC] * one, y[:, LC:]], axis=1)
        mu = jnp.mean(y, axis=-1, keepdims=True)
        yc = y - mu
        var = jnp.mean(yc * yc, axis=-1, keepdims=True)
        yn = yc * lax.rsqrt(var + EPS) * lg + lb
        out = yn * jax.nn.sigmoid(yn)
        tok = token(out[0:8, 0:LC])
        o_ref[q:q + NORM_ROWS, :] = out.astype(o_ref.dtype)
    tok = None
    for c in range(CONV_DIM // LC):
        ls = slice(c * LC, (c + 1) * LC)
        for r0 in range(0, CONV_ROWS, R):
            acc = None
            for b in range(SUB):
                p = None
                for a8 in range(-(-CONV_WIDTH // SUB)):
                    j = SUB * a8 + b
                    if j >= CONV_WIDTH:
                        continue
                    start = r0 + CONV_HIST - SUB - SUB * a8
                    k = CONV_WIDTH - 1 - j
                    w = w_ref[k:k + 1, ls]
                    if p is None:
                        w = after(w, tok)
                    term = win_ref[start:start + R + SUB, ls] * w
                    p = term if p is None else p + term
                if b:
                    p = pltpu.roll(p, b, 0)
                p = p[SUB:]
                acc = p if acc is None else acc + p
            tok = token(acc[0:8])
            y_ref[r0:r0 + R, ls] = acc + b_ref[:, ls]


GLU_STEP = 2
CONV_FIRST_STEP = 3
NORM_FIRST_STEP = 4


def _block_of_step(j, first_step, n_blocks):
    return jnp.clip(j - first_step, 0, n_blocks - 1)


def _inproj_kernel(*refs, tiles_per_seq, cast_steps):
    n_cast = len(cast_steps)
    x_ref, g_ref, sh_ref, sc_ref, w_ref, cw_ref, cb_ref, lg_ref, lb_ref, zero_ref = refs[:10]
    cast_in = refs[10:10 + n_cast]
    o_ref, ac_ref = refs[10 + n_cast:12 + n_cast]
    cast_out = refs[12 + n_cast:12 + 2 * n_cast]
    h_ref, stash_ref, vext_ref, win_ref, y_ref = refs[12 + 2 * n_cast:]
    i = pl.program_id(0)
    j = pl.program_id(1)
    tm = x_ref.shape[0]

    for steps in sorted(set(cast_steps)):
        @pl.when(j < steps)
        def _(steps=steps):
            for src, dst, n in zip(cast_in, cast_out, cast_steps):
                if n == steps:
                    dst[...] = src[...].astype(dst.dtype)

    @pl.when(j == 0)
    def _():
        _rms_modulate_into(x_ref, h_ref, g_ref[...], sh_ref[0], sc_ref[0])

    @pl.when(jnp.logical_and(i == 0, j == 0))
    def _():
        vext_ref[...] = jnp.zeros(vext_ref.shape, F32)
        win_ref[...] = jnp.zeros(win_ref.shape, F32)
        y_ref[...] = jnp.zeros(y_ref.shape, F32)

    @pl.when(j == GLU_STEP)
    def _():
        first = i % tiles_per_seq == 0

        @pl.when(first)
        def _():
            vext_ref[0:CONV_HIST, :] = jnp.zeros((CONV_HIST, CONV_DIM), F32)

        @pl.when(jnp.logical_not(first))
        def _():
            vext_ref[0:CONV_HIST, :] = vext_ref[tm:tm + CONV_HIST, :]

        a = stash_ref[0].astype(F32)
        gt = stash_ref[1].astype(F32)
        vext_ref[CONV_HIST:CONV_HIST + tm, :] = a * jax.nn.sigmoid(gt)

    _conv_rows(win_ref, y_ref, ac_ref, cw_ref, cb_ref, lg_ref[...], lb_ref[...], zero_ref)

    o_ref[...] = jnp.dot(h_ref[...], w_ref[...], preferred_element_type=F32).astype(o_ref.dtype)

    nxt = _block_of_step(j + 1, CONV_FIRST_STEP, tm // CONV_ROWS)
    win_ref[...] = vext_ref[pl.ds(pl.multiple_of(nxt * CONV_ROWS, CONV_ROWS), CONV_HIST + CONV_ROWS), :]

    @pl.when(j < 2)
    def _():
        stash_ref[j] = o_ref[...]


BF16_SUBLANES = 16


def _inproj(x2d, g, shift, scale, w, conv_w, conv_b, ln_g, ln_b, seq, to_cast):
    M, D = x2d.shape
    N = w.shape[1]
    C = CONV_DIM
    tm, tn = 1024, 1024
    per_b = seq // tm
    n_blocks = tm // CONV_ROWS
    n_i, n_j = M // tm, N // tn
    assert tn == C and n_j >= NORM_FIRST_STEP + n_blocks
    const = lambda i, j: (0, 0)
    cast_steps, cast_specs = [], []
    for wk in to_cast:
        steps = 1
        while steps * 2 <= n_j and wk.shape[0] % (n_i * steps * 2 * BF16_SUBLANES) == 0:
            steps *= 2
        rows = wk.shape[0] // (n_i * steps)
        assert rows % BF16_SUBLANES == 0
        cast_steps.append(steps)
        cast_specs.append(((rows, wk.shape[1]),
                           lambda i, j, steps=steps: (i * steps + jnp.minimum(j, steps - 1), 0)))
    return pl.pallas_call(
        functools.partial(_inproj_kernel, tiles_per_seq=per_b, cast_steps=tuple(cast_steps)),
        out_shape=(jax.ShapeDtypeStruct((M, N), BF16), jax.ShapeDtypeStruct((M, C), BF16),
                   *(jax.ShapeDtypeStruct(wk.shape, BF16) for wk in to_cast)),
        grid=(n_i, n_j),
        in_specs=[
            pl.BlockSpec((tm, D), lambda i, j: (i, 0)),
            pl.BlockSpec((1, D), const),
            pl.BlockSpec((1, 1, D), lambda i, j: (i // per_b, 0, 0)),
            pl.BlockSpec((1, 1, D), lambda i, j: (i // per_b, 0, 0)),
            pl.BlockSpec((D, tn), lambda i, j: (0, j)),
            pl.BlockSpec((CONV_WIDTH, C), const),
            pl.BlockSpec((1, C), const),
            pl.BlockSpec((1, C), const),
            pl.BlockSpec((1, C), const),
            pl.BlockSpec((1, CONV_LANES), const),
            *(pl.BlockSpec(shape, index) for shape, index in cast_specs),
        ],
        out_specs=(pl.BlockSpec((tm, tn), lambda i, j: (i, j)),
                   pl.BlockSpec((CONV_ROWS, C),
                                lambda i, j: (i * n_blocks + _block_of_step(j, NORM_FIRST_STEP, n_blocks), 0)),
                   *(pl.BlockSpec(shape, index) for shape, index in cast_specs)),
        scratch_shapes=[
            pltpu.VMEM((tm, D), BF16),
            pltpu.VMEM((2, tm, C), BF16),
            pltpu.VMEM((CONV_HIST + tm, C), F32),
            pltpu.VMEM((CONV_HIST + CONV_ROWS, C), F32),
            pltpu.VMEM((CONV_ROWS, C), F32),
        ],
        compiler_params=_params(("arbitrary", "arbitrary")),
        name="in_proj",
    )(x2d, g, shift, scale, w, conv_w, conv_b.reshape(1, C), ln_g.reshape(1, C), ln_b.reshape(1, C),
      jnp.zeros((1, CONV_LANES), jnp.int32), *to_cast)


def _ret_kernel(pos_ref, q_ref, k_ref, v_ref, g_ref, invf_ref, sgn_ref, dq_ref, dk_ref,
                cd_ref, gng_ref, gnb_ref, o_ref, r_ref, *, ts):
    s = pl.program_id(1)

    @pl.when(s == 0)
    def _():
        r_ref[...] = jnp.zeros(r_ref.shape, F32)

    ang = pos_ref[...].astype(F32) * invf_ref[...]
    cosf = jnp.cos(ang)
    sinf = jnp.sin(ang) * sgn_ref[...]
    C, SB = RET_CHUNK, RET_ROWS
    causal = []
    for m in range(C // SB):
        row = lax.broadcasted_iota(jnp.int32, (SB, (m + 1) * SB), 0) + m * SB
        col = lax.broadcasted_iota(jnp.int32, (SB, (m + 1) * SB), 1)
        causal.append(col <= row)
    for n in range(ts // C):
        rs = slice(n * C, (n + 1) * C)
        cs_, sn_ = cosf[rs], sinf[rs]
        qds, kds, ps = [], [], []
        for h in range(RET_HEADS):
            qs = slice(h * QK_HEAD, (h + 1) * QK_HEAD)
            q = q_ref[rs, qs].astype(F32)
            k = k_ref[rs, qs].astype(F32)
            qd = ((q * cs_ + pltpu.roll(q, QK_HEAD // 2, 1) * sn_) * dq_ref[h]).astype(BF16)
            kd = ((k * cs_ + pltpu.roll(k, QK_HEAD // 2, 1) * sn_) * dk_ref[h]).astype(BF16)
            qds.append(qd)
            kds.append(kd)
            for m in range(C // SB):
                kcols = (m + 1) * SB
                sc = lax.dot_general(qd[m * SB:(m + 1) * SB], kd[:kcols], (((1,), (1,)), ((), ())),
                                     preferred_element_type=F32)
                ps.append(jnp.where(causal[m], sc, 0.0).astype(BF16))
        for h in range(RET_HEADS):
            vs = slice(h * V_HEAD, (h + 1) * V_HEAD)
            v = v_ref[rs, vs]
            Rb = r_ref[h].astype(BF16)
            gng = gng_ref[:, vs]
            gnb = gnb_ref[:, vs]
            for m in range(C // SB):
                kcols = (m + 1) * SB
                lhs = jnp.concatenate([ps[h * (C // SB) + m], qds[h][m * SB:(m + 1) * SB]], axis=1)
                rhs = jnp.concatenate([v[:kcols], Rb], axis=0)
                o = jnp.dot(lhs, rhs, preferred_element_type=F32)
                mu = jnp.mean(o, axis=-1, keepdims=True)
                oc = o - mu
                var = jnp.mean(oc * oc, axis=-1, keepdims=True)
                on = oc * lax.rsqrt(var + EPS) * gng + gnb
                orow = slice(n * C + m * SB, n * C + (m + 1) * SB)
                g = g_ref[orow, vs].astype(F32)
                o_ref[orow, vs] = (g * jax.nn.sigmoid(g) * on).astype(o_ref.dtype)
        for h in range(RET_HEADS):
            vs = slice(h * V_HEAD, (h + 1) * V_HEAD)
            kv = lax.dot_general(kds[h], v_ref[rs, vs], (((0,), (0,)), ((), ())),
                                 preferred_element_type=F32)
            r_ref[h] = (r_ref[h] + kv) * cd_ref[h]


def _ret_tables():
    C = RET_CHUNK
    log_gamma = jnp.log(1.0 - jnp.exp2(-5.0 - jnp.arange(RET_HEADS, dtype=F32)))
    idx = jnp.arange(C, dtype=F32)
    dq = jnp.exp((idx + 1.0)[None, :] * log_gamma[:, None])
    dk = jnp.exp(-(idx + 1.0)[None, :] * log_gamma[:, None]) * (QK_HEAD ** -0.5)
    cd = jnp.exp(C * log_gamma)
    half = QK_HEAD // 2
    inv_freq = ROPE_BASE ** (-jnp.arange(0, half, dtype=F32) / half)
    invf = jnp.concatenate([inv_freq, inv_freq]).reshape(1, QK_HEAD)
    sgn = jnp.concatenate([-jnp.ones((half,), F32), jnp.ones((half,), F32)]).reshape(1, QK_HEAD)
    cd_b = jnp.broadcast_to(cd[:, None, None], (RET_HEADS, 1, V_HEAD))
    return invf, sgn, dq[:, :, None], dk[:, :, None], cd_b


def _ret(proj, positions, gn_g, gn_b, batch, seq):
    M = proj.shape[0]
    ts = 512
    per_b = seq // ts
    C = RET_CHUNK
    invf, sgn, dq, dk, cd = _ret_tables()
    row = lambda b, s: b * per_b + s
    const2 = lambda b, s: (0, 0)
    const3 = lambda b, s: (0, 0, 0)
    return pl.pallas_call(
        functools.partial(_ret_kernel, ts=ts),
        out_shape=jax.ShapeDtypeStruct((M, RET_V_DIM), BF16),
        grid=(batch, per_b),
        in_specs=[
            pl.BlockSpec((ts, 1), lambda b, s: (row(b, s), 0)),
            pl.BlockSpec((ts, RET_QK_DIM), lambda b, s: (row(b, s), 2)),
            pl.BlockSpec((ts, RET_QK_DIM), lambda b, s: (row(b, s), 3)),
            pl.BlockSpec((ts, RET_V_DIM), lambda b, s: (row(b, s), 2)),
            pl.BlockSpec((ts, RET_V_DIM), lambda b, s: (row(b, s), 3)),
            pl.BlockSpec((1, QK_HEAD), const2),
            pl.BlockSpec((1, QK_HEAD), const2),
            pl.BlockSpec((RET_HEADS, C, 1), const3),
            pl.BlockSpec((RET_HEADS, C, 1), const3),
            pl.BlockSpec((RET_HEADS, 1, V_HEAD), const3),
            pl.BlockSpec((1, RET_V_DIM), const2),
            pl.BlockSpec((1, RET_V_DIM), const2),
        ],
        out_specs=pl.BlockSpec((ts, RET_V_DIM), lambda b, s: (row(b, s), 0)),
        scratch_shapes=[pltpu.VMEM((RET_HEADS, QK_HEAD, V_HEAD), F32)],
        compiler_params=_params(("parallel", "arbitrary")),
        name="retention",
    )(positions.reshape(M, 1), proj, proj, proj, proj, invf, sgn, dq, dk, cd,
      gn_g.reshape(1, RET_V_DIM), gn_b.reshape(1, RET_V_DIM))


def _mixout_kernel(ac_ref, ar_ref, gc_ref, gr_ref, x_ref, gm_ref, wc_ref, wr_ref, wo_ref, o_ref):
    yc = jnp.dot(ac_ref[...], wc_ref[...], preferred_element_type=F32)
    yr = jnp.dot(ar_ref[...], wr_ref[...], preferred_element_type=F32)
    merged = (jax.nn.sigmoid(gc_ref[...].astype(F32)) * yc
              + jax.nn.sigmoid(gr_ref[...].astype(F32)) * yr)
    out = jnp.dot(merged.astype(BF16), wo_ref[...], preferred_element_type=F32)
    o_ref[...] = x_ref[...] + gm_ref[0] * out


def _mixout(a_conv, a_ret, proj, x2d, gate_m, wc, wr, wo, seq):
    M, D = x2d.shape
    tm = 256
    per_b = seq // tm
    resident = functools.partial(pl.BlockSpec, pipeline_mode=pl.Buffered(1))
    return pl.pallas_call(
        _mixout_kernel,
        out_shape=jax.ShapeDtypeStruct((M, D), F32),
        grid=(M // tm,),
        in_specs=[
            pl.BlockSpec((tm, CONV_DIM), lambda i: (i, 0)),
            pl.BlockSpec((tm, RET_V_DIM), lambda i: (i, 0)),
            pl.BlockSpec((tm, D), lambda i: (i, 4)),
            pl.BlockSpec((tm, D), lambda i: (i, 5)),
            pl.BlockSpec((tm, D), lambda i: (i, 0)),
            pl.BlockSpec((1, 1, D), lambda i: (i // per_b, 0, 0)),
            resident((CONV_DIM, D), lambda i: (0, 0)),
            resident((RET_V_DIM, D), lambda i: (0, 0)),
            resident((D, D), lambda i: (0, 0)),
        ],
        out_specs=pl.BlockSpec((tm, D), lambda i: (i, 0)),
        compiler_params=_params(("parallel",)),
        name="mix_out",
    )(a_conv, a_ret, proj, proj, x2d, gate_m, wc, wr, wo)


def _ffn_kernel(x_ref, g_ref, sh_ref, sc_ref, gf_ref, w1_ref, w2_ref, gfin_ref, o_ref, h_ref):
    j = pl.program_id(1)

    @pl.when(j == 0)
    def _():
        _rms_modulate_into(x_ref, h_ref, g_ref[...], sh_ref[0], sc_ref[0])
        o_ref[...] = jnp.zeros(o_ref.shape, F32)

    hid = jnp.dot(h_ref[...], w1_ref[...], preferred_element_type=F32)
    hid = jnp.square(jnp.maximum(hid, 0.0)).astype(BF16)
    o_ref[...] += jnp.dot(hid, w2_ref[...], preferred_element_type=F32)

    @pl.when(j == pl.num_programs(1) - 1)
    def _():
        gf = gf_ref[0]
        gfin = gfin_ref[...]

        def body(r, carry):
            base = r * (NORM_ROWS * NORM_UNROLL)
            rows = [pl.ds(pl.multiple_of(base + u * NORM_ROWS, NORM_ROWS), NORM_ROWS)
                    for u in range(NORM_UNROLL)]
            outs = []
            for rs in rows:
                x2 = x_ref[rs, :] + gf * o_ref[rs, :]
                ms = jnp.mean(x2 * x2, axis=-1, keepdims=True)
                outs.append(x2 * lax.rsqrt(ms + EPS) * gfin)
            for rs, out in zip(rows, outs):
                o_ref[rs, :] = out
            return carry

        lax.fori_loop(0, x_ref.shape[0] // (NORM_ROWS * NORM_UNROLL), body, 0)


def _ffn(x1, g, shift, scale, gate, w1, w2, g_final, seq):
    M, D = x1.shape
    F = w1.shape[1]
    tm, tf = 1024, 1024
    per_b = seq // tm
    bidx = lambda i, j: (i // per_b, 0, 0)
    return pl.pallas_call(
        _ffn_kernel,
        out_shape=jax.ShapeDtypeStruct((M, D), F32),
        grid=(M // tm, F // tf),
        in_specs=[
            pl.BlockSpec((tm, D), lambda i, j: (i, 0)),
            pl.BlockSpec((1, D), lambda i, j: (0, 0)),
            pl.BlockSpec((1, 1, D), bidx),
            pl.BlockSpec((1, 1, D), bidx),
            pl.BlockSpec((1, 1, D), bidx),
            pl.BlockSpec((D, tf), lambda i, j: (0, j)),
            pl.BlockSpec((tf, D), lambda i, j: (j, 0)),
            pl.BlockSpec((1, D), lambda i, j: (0, 0)),
        ],
        out_specs=pl.BlockSpec((tm, D), lambda i, j: (i, 0)),
        scratch_shapes=[pltpu.VMEM((tm, D), BF16)],
        compiler_params=_params(("parallel", "arbitrary"), vmem=FFN_VMEM_LIMIT),
        name="ffn",
    )(x1, g, shift, scale, gate, w1, w2, g_final)


def kernel(x, c, positions, w_ada, b_ada, g_norm_mix, w_in, conv_w, conv_b, conv_ln_g, conv_ln_b,
           w_conv_out, ret_gn_g, ret_gn_b, w_ret_out, w_out, g_norm_ffn, w_ff1, w_ff2, g_norm_final):
    B, S, D = x.shape
    assert w_ada.shape[0] == 1, "kernel supports DEPTH == 1"
    l = 0
    x2d = x.reshape(B * S, D)
    mod = _mod(c, w_ada[l], b_ada[l]).reshape(B, N_MOD, 1, D)
    shift_m, scale_m, gate_m, shift_f, scale_f, gate_f = (mod[:, i] for i in range(N_MOD))
    proj, a_conv, wc, wr, wo, w1, w2 = _inproj(
        x2d, g_norm_mix[l].reshape(1, D), shift_m, scale_m, w_in[l].astype(BF16),
        conv_w[l], conv_b[l], conv_ln_g[l], conv_ln_b[l], S,
        to_cast=(w_conv_out[l], w_ret_out[l], w_out[l], w_ff1[l], w_ff2[l]))
    a_ret = _ret(proj, positions, ret_gn_g[l], ret_gn_b[l], B, S)
    x1 = _mixout(a_conv, a_ret, proj, x2d, gate_m, wc, wr, wo, S)
    out = _ffn(x1, g_norm_ffn[l].reshape(1, D), shift_f, scale_f, gate_f, w1, w2,
               g_norm_final.reshape(1, D), S)
    return out.reshape(B, S, D)
```

```python
import functools

import jax
import jax.numpy as jnp
from jax import lax
from jax.experimental import pallas as pl
from jax.experimental.pallas import tpu as pltpu

D_MODEL = 2048
CONV_DIM = D_MODEL // 2
CONV_WIDTH = 31
RET_HEADS = 8
RET_QK_DIM = D_MODEL // 2
RET_V_DIM = D_MODEL
QK_HEAD = RET_QK_DIM // RET_HEADS
V_HEAD = RET_V_DIM // RET_HEADS
D_FF = 4 * D_MODEL
ROPE_BASE = 10000.0
EPS = 1e-6
IN_COLS = 2 * CONV_DIM + 2 * RET_QK_DIM + 2 * RET_V_DIM + 2 * D_MODEL
N_MOD = 6

RET_CHUNK = 256
RET_ROWS = 128
CONV_HIST = 32
CONV_ROWS = 128
CONV_PIECE_ROWS = 32
CONV_LANES = 128

VMEM_LIMIT = 56 * 1024 * 1024
FFN_VMEM_LIMIT = 60 * 1024 * 1024
BF16 = jnp.bfloat16
F32 = jnp.float32


def _params(sem, vmem=VMEM_LIMIT):
    return pltpu.CompilerParams(dimension_semantics=sem, vmem_limit_bytes=vmem)


def _mod_kernel(c_ref, w_ref, b_ref, o_ref):
    c = c_ref[...]
    ca = (c * jax.nn.sigmoid(c)).astype(BF16)
    o_ref[...] = jnp.dot(ca, w_ref[...].astype(BF16), preferred_element_type=F32) + b_ref[...]


def _mod(c, w_ada, b_ada):
    B, D = c.shape
    N = w_ada.shape[1]
    tn = 1024
    return pl.pallas_call(
        _mod_kernel,
        out_shape=jax.ShapeDtypeStruct((B, N), F32),
        grid=(N // tn,),
        in_specs=[
            pl.BlockSpec((B, D), lambda j: (0, 0)),
            pl.BlockSpec((D, tn), lambda j: (0, j)),
            pl.BlockSpec((1, tn), lambda j: (0, j)),
        ],
        out_specs=pl.BlockSpec((B, tn), lambda j: (0, j)),
        compiler_params=_params(("arbitrary",)),
        name="mod",
    )(c, w_ada, b_ada.reshape(1, N))


NORM_ROWS = 16
NORM_UNROLL = 8


def _row_loop(n_rows, fn):
    def body(r, carry):
        fn(pl.multiple_of(r * NORM_ROWS, NORM_ROWS))
        return carry

    lax.fori_loop(0, n_rows // NORM_ROWS, body, 0, unroll=NORM_UNROLL)


def _rms_modulate_into(x_ref, h_ref, g, shift, scale):
    gs = g * (1.0 + scale)

    def rows(r0):
        x = x_ref[pl.ds(r0, NORM_ROWS), :]
        ms = jnp.mean(x * x, axis=-1, keepdims=True)
        h_ref[pl.ds(r0, NORM_ROWS), :] = (x * lax.rsqrt(ms + EPS) * gs + shift).astype(h_ref.dtype)

    _row_loop(x_ref.shape[0], rows)


def _conv_rows(win_ref, y_ref, o_ref, w_ref, b_ref, lg, lb, zero_ref):
    SUB = 8
    R, LC = CONV_PIECE_ROWS, CONV_LANES
    zero = zero_ref[...]

    def token(v):
        return lax.bitcast_convert_type(pltpu.roll(v, 1, 1), jnp.int32) & zero

    def after(x, tok):
        if tok is None:
            return x
        return lax.bitcast_convert_type(lax.bitcast_convert_type(x, jnp.int32) | tok[0:x.shape[0]], F32)

    one_bits = jnp.int32(0x3F800000)
    tok = None
    for q in range(0, CONV_ROWS, NORM_ROWS):
        y = y_ref[q:q + NORM_ROWS, :]
        if tok is not None:
            one = lax.bitcast_convert_type(tok[0:1] | one_bits, F32)
            y = jnp.concatenate([y[:, 0:LC] * one, y[:, LC:]], axis=1)
        mu = jnp.mean(y, axis=-1, keepdims=True)
        yc = y - mu
        var = jnp.mean(yc * yc, axis=-1, keepdims=True)
        yn = yc * lax.rsqrt(var + EPS) * lg + lb
        out = yn * jax.nn.sigmoid(yn)
        tok = token(out[0:8, 0:LC])
        o_ref[q:q + NORM_ROWS, :] = out.astype(o_ref.dtype)
    tok = None
    for c in range(CONV_DIM // LC):
        ls = slice(c * LC, (c + 1) * LC)
        for r0 in range(0, CONV_ROWS, R):
            acc = None
            for b in range(SUB):
                p = None
                for a8 in range(-(-CONV_WIDTH // SUB)):
                    j = SUB * a8 + b
                    if j >= CONV_WIDTH:
                        continue
                    start = r0 + CONV_HIST - SUB - SUB * a8
                    k = CONV_WIDTH - 1 - j
                    w = w_ref[k:k + 1, ls]
                    if p is None:
                        w = after(w, tok)
                    term = win_ref[start:start + R + SUB, ls] * w
                    p = term if p is None else p + term
                if b:
                    p = pltpu.roll(p, b, 0)
                p = p[SUB:]
                acc = p if acc is None else acc + p
            tok = token(acc[0:8])
            y_ref[r0:r0 + R, ls] = acc + b_ref[:, ls]


GLU_STEP = 2
CONV_FIRST_STEP = 3
NORM_FIRST_STEP = 4


def _block_of_step(j, first_step, n_blocks):
    return jnp.clip(j - first_step, 0, n_blocks - 1)


def _inproj_kernel(*refs, tiles_per_seq, cast_steps):
    n_cast = len(cast_steps)
    x_ref, g_ref, sh_ref, sc_ref, w_ref, cw_ref, cb_ref, lg_ref, lb_ref, zero_ref = refs[:10]
    cast_in = refs[10:10 + n_cast]
    o_ref, ac_ref = refs[10 + n_cast:12 + n_cast]
    cast_out = refs[12 + n_cast:12 + 2 * n_cast]
    h_ref, stash_ref, vext_ref, win_ref, y_ref = refs[12 + 2 * n_cast:]
    i = pl.program_id(0)
    j = pl.program_id(1)
    tm = x_ref.shape[0]

    for steps in sorted(set(cast_steps)):
        @pl.when(j < steps)
        def _(steps=steps):
            for src, dst, n in zip(cast_in, cast_out, cast_steps):
                if n == steps:
                    dst[...] = src[...].astype(dst.dtype)

    @pl.when(j == 0)
    def _():
        _rms_modulate_into(x_ref, h_ref, g_ref[...], sh_ref[0], sc_ref[0])

    @pl.when(jnp.logical_and(i == 0, j == 0))
    def _():
        vext_ref[...] = jnp.zeros(vext_ref.shape, F32)
        win_ref[...] = jnp.zeros(win_ref.shape, F32)
        y_ref[...] = jnp.zeros(y_ref.shape, F32)

    @pl.when(j == GLU_STEP)
    def _():
        first = i % tiles_per_seq == 0

        @pl.when(first)
        def _():
            vext_ref[0:CONV_HIST, :] = jnp.zeros((CONV_HIST, CONV_DIM), F32)

        @pl.when(jnp.logical_not(first))
        def _():
            vext_ref[0:CONV_HIST, :] = vext_ref[tm:tm + CONV_HIST, :]

        a = stash_ref[0].astype(F32)
        gt = stash_ref[1].astype(F32)
        vext_ref[CONV_HIST:CONV_HIST + tm, :] = a * jax.nn.sigmoid(gt)

    _conv_rows(win_ref, y_ref, ac_ref, cw_ref, cb_ref, lg_ref[...], lb_ref[...], zero_ref)

    o_ref[...] = jnp.dot(h_ref[...], w_ref[...], preferred_element_type=F32).astype(o_ref.dtype)

    nxt = _block_of_step(j + 1, CONV_FIRST_STEP, tm // CONV_ROWS)
    win_ref[...] = vext_ref[pl.ds(pl.multiple_of(nxt * CONV_ROWS, CONV_ROWS), CONV_HIST + CONV_ROWS), :]

    @pl.when(j < 2)
    def _():
        stash_ref[j] = o_ref[...]


BF16_SUBLANES = 16


def _inproj(x2d, g, shift, scale, w, conv_w, conv_b, ln_g, ln_b, seq, to_cast):
    M, D = x2d.shape
    N = w.shape[1]
    C = CONV_DIM
    tm, tn = 1024, 1024
    per_b = seq // tm
    n_blocks = tm // CONV_ROWS
    n_i, n_j = M // tm, N // tn
    assert tn == C and n_j >= NORM_FIRST_STEP + n_blocks
    const = lambda i, j: (0, 0)
    cast_steps, cast_specs = [], []
    for wk in to_cast:
        steps = 1
        while steps * 2 <= n_j and wk.shape[0] % (n_i * steps * 2 * BF16_SUBLANES) == 0:
            steps *= 2
        rows = wk.shape[0] // (n_i * steps)
        assert rows % BF16_SUBLANES == 0
        cast_steps.append(steps)
        cast_specs.append(((rows, wk.shape[1]),
                           lambda i, j, steps=steps: (i * steps + jnp.minimum(j, steps - 1), 0)))
    return pl.pallas_call(
        functools.partial(_inproj_kernel, tiles_per_seq=per_b, cast_steps=tuple(cast_steps)),
        out_shape=(jax.ShapeDtypeStruct((M, N), BF16), jax.ShapeDtypeStruct((M, C), BF16),
                   *(jax.ShapeDtypeStruct(wk.shape, BF16) for wk in to_cast)),
        grid=(n_i, n_j),
        in_specs=[
            pl.BlockSpec((tm, D), lambda i, j: (i, 0)),
            pl.BlockSpec((1, D), const),
            pl.BlockSpec((1, 1, D), lambda i, j: (i // per_b, 0, 0)),
            pl.BlockSpec((1, 1, D), lambda i, j: (i // per_b, 0, 0)),
            pl.BlockSpec((D, tn), lambda i, j: (0, j)),
            pl.BlockSpec((CONV_WIDTH, C), const),
            pl.BlockSpec((1, C), const),
            pl.BlockSpec((1, C), const),
            pl.BlockSpec((1, C), const),
            pl.BlockSpec((1, CONV_LANES), const),
            *(pl.BlockSpec(shape, index) for shape, index in cast_specs),
        ],
        out_specs=(pl.BlockSpec((tm, tn), lambda i, j: (i, j)),
                   pl.BlockSpec((CONV_ROWS, C),
                                lambda i, j: (i * n_blocks + _block_of_step(j, NORM_FIRST_STEP, n_blocks), 0)),
                   *(pl.BlockSpec(shape, index) for shape, index in cast_specs)),
        scratch_shapes=[
            pltpu.VMEM((tm, D), BF16),
            pltpu.VMEM((2, tm, C), BF16),
            pltpu.VMEM((CONV_HIST + tm, C), F32),
            pltpu.VMEM((CONV_HIST + CONV_ROWS, C), F32),
            pltpu.VMEM((CONV_ROWS, C), F32),
        ],
        compiler_params=_params(("arbitrary", "arbitrary")),
        name="in_proj",
    )(x2d, g, shift, scale, w, conv_w, conv_b.reshape(1, C), ln_g.reshape(1, C), ln_b.reshape(1, C),
      jnp.zeros((1, CONV_LANES), jnp.int32), *to_cast)


def _ret_kernel(pos_ref, q_ref, k_ref, v_ref, g_ref, invf_ref, sgn_ref, dq_ref, dk_ref,
                cd_ref, gng_ref, gnb_ref, o_ref, r_ref, *, ts):
    s = pl.program_id(1)

    @pl.when(s == 0)
    def _():
        r_ref[...] = jnp.zeros(r_ref.shape, F32)

    half = QK_HEAD // 2
    pos = pos_ref[...].astype(F32)
    invf = invf_ref[...]
    low = lax.broadcasted_iota(jnp.int32, (ts // 2, QK_HEAD), 1) < half
    ang = jnp.where(low, pos[:ts // 2] * invf, pos[ts // 2:] * invf)
    c2, s2 = jnp.cos(ang), jnp.sin(ang)
    c2r, s2r = pltpu.roll(c2, half, 1), pltpu.roll(s2, half, 1)
    cosf = jnp.concatenate([jnp.where(low, c2, c2r), jnp.where(low, c2r, c2)], axis=0)
    sinf = jnp.concatenate([jnp.where(low, s2, s2r), jnp.where(low, s2r, s2)], axis=0) * sgn_ref[...]
    C, SB = RET_CHUNK, RET_ROWS
    causal = []
    for m in range(C // SB):
        row = lax.broadcasted_iota(jnp.int32, (SB, (m + 1) * SB), 0) + m * SB
        col = lax.broadcasted_iota(jnp.int32, (SB, (m + 1) * SB), 1)
        causal.append(col <= row)
    for n in range(ts // C):
        rs = slice(n * C, (n + 1) * C)
        cs_, sn_ = cosf[rs], sinf[rs]
        qds, kds, ps = [], [], []
        for h in range(RET_HEADS):
            qs = slice(h * QK_HEAD, (h + 1) * QK_HEAD)
            q = q_ref[rs, qs].astype(F32)
            k = k_ref[rs, qs].astype(F32)
            qd = ((q * cs_ + pltpu.roll(q, QK_HEAD // 2, 1) * sn_) * dq_ref[h]).astype(BF16)
            kd = ((k * cs_ + pltpu.roll(k, QK_HEAD // 2, 1) * sn_) * dk_ref[h]).astype(BF16)
            qds.append(qd)
            kds.append(kd)
            for m in range(C // SB):
                kcols = (m + 1) * SB
                sc = lax.dot_general(qd[m * SB:(m + 1) * SB], kd[:kcols], (((1,), (1,)), ((), ())),
                                     preferred_element_type=F32)
                ps.append(jnp.where(causal[m], sc, 0.0).astype(BF16))
        for h in range(RET_HEADS):
            vs = slice(h * V_HEAD, (h + 1) * V_HEAD)
            v = v_ref[rs, vs]
            Rb = r_ref[h].astype(BF16)
            gng = gng_ref[:, vs]
            gnb = gnb_ref[:, vs]
            for m in range(C // SB):
                kcols = (m + 1) * SB
                lhs = jnp.concatenate([ps[h * (C // SB) + m], qds[h][m * SB:(m + 1) * SB]], axis=1)
                rhs = jnp.concatenate([v[:kcols], Rb], axis=0)
                o = jnp.dot(lhs, rhs, preferred_element_type=F32)
                mu = jnp.mean(o, axis=-1, keepdims=True)
                oc = o - mu
                var = jnp.mean(oc * oc, axis=-1, keepdims=True)
                on = oc * lax.rsqrt(var + EPS) * gng + gnb
                orow = slice(n * C + m * SB, n * C + (m + 1) * SB)
                g = g_ref[orow, vs]
                o_ref[orow, vs] = g * jax.nn.sigmoid(g) * on.astype(o_ref.dtype)
        for h in range(RET_HEADS):
            vs = slice(h * V_HEAD, (h + 1) * V_HEAD)
            kv = lax.dot_general(kds[h], v_ref[rs, vs], (((0,), (0,)), ((), ())),
                                 preferred_element_type=F32)
            r_ref[h] = (r_ref[h] + kv) * cd_ref[h]


def _ret_tables():
    C = RET_CHUNK
    log_gamma = jnp.log(1.0 - jnp.exp2(-5.0 - jnp.arange(RET_HEADS, dtype=F32)))
    idx = jnp.arange(C, dtype=F32)
    dq = jnp.exp((idx + 1.0)[None, :] * log_gamma[:, None])
    dk = jnp.exp(-(idx + 1.0)[None, :] * log_gamma[:, None]) * (QK_HEAD ** -0.5)
    cd = jnp.exp(C * log_gamma)
    half = QK_HEAD // 2
    inv_freq = ROPE_BASE ** (-jnp.arange(0, half, dtype=F32) / half)
    invf = jnp.concatenate([inv_freq, inv_freq]).reshape(1, QK_HEAD)
    sgn = jnp.concatenate([-jnp.ones((half,), F32), jnp.ones((half,), F32)]).reshape(1, QK_HEAD)
    cd_b = jnp.broadcast_to(cd[:, None, None], (RET_HEADS, 1, V_HEAD))
    lanes = (RET_HEADS, C, QK_HEAD)
    return (invf, sgn, jnp.broadcast_to(dq[:, :, None], lanes), jnp.broadcast_to(dk[:, :, None], lanes), cd_b)


def _ret(proj, positions, gn_g, gn_b, batch, seq):
    M = proj.shape[0]
    ts = 512
    per_b = seq // ts
    C = RET_CHUNK
    invf, sgn, dq, dk, cd = _ret_tables()
    row = lambda b, s: b * per_b + s
    const2 = lambda b, s: (0, 0)
    const3 = lambda b, s: (0, 0, 0)
    return pl.pallas_call(
        functools.partial(_ret_kernel, ts=ts),
        out_shape=jax.ShapeDtypeStruct((M, RET_V_DIM), BF16),
        grid=(batch, per_b),
        in_specs=[
            pl.BlockSpec((ts, 1), lambda b, s: (row(b, s), 0)),
            pl.BlockSpec((ts, RET_QK_DIM), lambda b, s: (row(b, s), 2)),
            pl.BlockSpec((ts, RET_QK_DIM), lambda b, s: (row(b, s), 3)),
            pl.BlockSpec((ts, RET_V_DIM), lambda b, s: (row(b, s), 2)),
            pl.BlockSpec((ts, RET_V_DIM), lambda b, s: (row(b, s), 3)),
            pl.BlockSpec((1, QK_HEAD), const2),
            pl.BlockSpec((1, QK_HEAD), const2),
            pl.BlockSpec((RET_HEADS, C, QK_HEAD), const3),
            pl.BlockSpec((RET_HEADS, C, QK_HEAD), const3),
            pl.BlockSpec((RET_HEADS, 1, V_HEAD), const3),
            pl.BlockSpec((1, RET_V_DIM), const2),
            pl.BlockSpec((1, RET_V_DIM), const2),
        ],
        out_specs=pl.BlockSpec((ts, RET_V_DIM), lambda b, s: (row(b, s), 0)),
        scratch_shapes=[pltpu.VMEM((RET_HEADS, QK_HEAD, V_HEAD), F32)],
        compiler_params=_params(("parallel", "arbitrary")),
        name="retention",
    )(positions.reshape(M, 1), proj, proj, proj, proj, invf, sgn, dq, dk, cd,
      gn_g.reshape(1, RET_V_DIM), gn_b.reshape(1, RET_V_DIM))


def _mixout_kernel(ac_ref, ar_ref, gc_ref, gr_ref, x_ref, gm_ref, wc_ref, wr_ref, wo_ref, o_ref):
    yc = jnp.dot(ac_ref[...], wc_ref[...], preferred_element_type=F32)
    yr = jnp.dot(ar_ref[...], wr_ref[...], preferred_element_type=F32)
    merged = (jax.nn.sigmoid(gc_ref[...].astype(F32)) * yc
              + jax.nn.sigmoid(gr_ref[...].astype(F32)) * yr)
    out = jnp.dot(merged.astype(BF16), wo_ref[...], preferred_element_type=F32)
    o_ref[...] = x_ref[...] + gm_ref[0] * out


def _mixout(a_conv, a_ret, proj, x2d, gate_m, wc, wr, wo, seq):
    M, D = x2d.shape
    tm = 256
    per_b = seq // tm
    resident = functools.partial(pl.BlockSpec, pipeline_mode=pl.Buffered(1))
    return pl.pallas_call(
        _mixout_kernel,
        out_shape=jax.ShapeDtypeStruct((M, D), F32),
        grid=(M // tm,),
        in_specs=[
            pl.BlockSpec((tm, CONV_DIM), lambda i: (i, 0)),
            pl.BlockSpec((tm, RET_V_DIM), lambda i: (i, 0)),
            pl.BlockSpec((tm, D), lambda i: (i, 4)),
            pl.BlockSpec((tm, D), lambda i: (i, 5)),
            pl.BlockSpec((tm, D), lambda i: (i, 0)),
            pl.BlockSpec((1, 1, D), lambda i: (i // per_b, 0, 0)),
            resident((CONV_DIM, D), lambda i: (0, 0)),
            resident((RET_V_DIM, D), lambda i: (0, 0)),
            resident((D, D), lambda i: (0, 0)),
        ],
        out_specs=pl.BlockSpec((tm, D), lambda i: (i, 0)),
        compiler_params=_params(("parallel",)),
        name="mix_out",
    )(a_conv, a_ret, proj, proj, x2d, gate_m, wc, wr, wo)


def _ffn_kernel(x_ref, g_ref, sh_ref, sc_ref, gf_ref, w1_ref, w2_ref, gfin_ref, o_ref, h_ref):
    j = pl.program_id(1)

    def chunk():
        hid = jnp.dot(h_ref[...], w1_ref[...], preferred_element_type=F32)
        hid = jnp.square(jnp.maximum(hid, 0.0)).astype(BF16)
        return jnp.dot(hid, w2_ref[...], preferred_element_type=F32)

    @pl.when(j == 0)
    def _():
        _rms_modulate_into(x_ref, h_ref, g_ref[...], sh_ref[0], sc_ref[0])
        o_ref[...] = chunk()

    @pl.when(j != 0)
    def _():
        o_ref[...] += chunk()

    @pl.when(j == pl.num_programs(1) - 1)
    def _():
        gf = gf_ref[0]
        gfin = gfin_ref[...]

        def body(r, carry):
            base = r * (NORM_ROWS * NORM_UNROLL)
            rows = [pl.ds(pl.multiple_of(base + u * NORM_ROWS, NORM_ROWS), NORM_ROWS)
                    for u in range(NORM_UNROLL)]
            outs = []
            for rs in rows:
                x2 = x_ref[rs, :] + gf * o_ref[rs, :]
                ms = jnp.mean(x2 * x2, axis=-1, keepdims=True)
                outs.append(x2 * lax.rsqrt(ms + EPS) * gfin)
            for rs, out in zip(rows, outs):
                o_ref[rs, :] = out
            return carry

        lax.fori_loop(0, x_ref.shape[0] // (NORM_ROWS * NORM_UNROLL), body, 0)


def _ffn(x1, g, shift, scale, gate, w1, w2, g_final, seq):
    M, D = x1.shape
    F = w1.shape[1]
    tm, tf = 1024, 1024
    per_b = seq // tm
    bidx = lambda i, j: (i // per_b, 0, 0)
    return pl.pallas_call(
        _ffn_kernel,
        out_shape=jax.ShapeDtypeStruct((M, D), F32),
        grid=(M // tm, F // tf),
        in_specs=[
            pl.BlockSpec((tm, D), lambda i, j: (i, 0)),
            pl.BlockSpec((1, D), lambda i, j: (0, 0)),
            pl.BlockSpec((1, 1, D), bidx),
            pl.BlockSpec((1, 1, D), bidx),
            pl.BlockSpec((1, 1, D), bidx),
            pl.BlockSpec((D, tf), lambda i, j: (0, j)),
            pl.BlockSpec((tf, D), lambda i, j: (j, 0)),
            pl.BlockSpec((1, D), lambda i, j: (0, 0)),
        ],
        out_specs=pl.BlockSpec((tm, D), lambda i, j: (i, 0)),
        scratch_shapes=[pltpu.VMEM((tm, D), BF16)],
        compiler_params=_params(("parallel", "arbitrary"), vmem=FFN_VMEM_LIMIT),
        name="ffn",
    )(x1, g, shift, scale, gate, w1, w2, g_final)


def kernel(x, c, positions, w_ada, b_ada, g_norm_mix, w_in, conv_w, conv_b, conv_ln_g, conv_ln_b,
           w_conv_out, ret_gn_g, ret_gn_b, w_ret_out, w_out, g_norm_ffn, w_ff1, w_ff2, g_norm_final):
    B, S, D = x.shape
    assert w_ada.shape[0] == 1, "kernel supports DEPTH == 1"
    l = 0
    x2d = x.reshape(B * S, D)
    mod = _mod(c, w_ada[l], b_ada[l]).reshape(B, N_MOD, 1, D)
    shift_m, scale_m, gate_m, shift_f, scale_f, gate_f = (mod[:, i] for i in range(N_MOD))
    proj, a_conv, wc, wr, wo, w1, w2 = _inproj(
        x2d, g_norm_mix[l].reshape(1, D), shift_m, scale_m, w_in[l].astype(BF16),
        conv_w[l], conv_b[l], conv_ln_g[l], conv_ln_b[l], S,
        to_cast=(w_conv_out[l], w_ret_out[l], w_out[l], w_ff1[l], w_ff2[l]))
    a_ret = _ret(proj, positions, ret_gn_g[l], ret_gn_b[l], B, S)
    x1 = _mixout(a_conv, a_ret, proj, x2d, gate_m, wc, wr, wo, S)
    out = _ffn(x1, g_norm_ffn[l].reshape(1, D), shift_f, scale_f, gate_f, w1, w2,
               g_norm_final.reshape(1, D), S)
    return out.reshape(B, S, D)
```

```python
import functools

import jax
import jax.numpy as jnp
from jax import lax
from jax.experimental import pallas as pl
from jax.experimental.pallas import tpu as pltpu

D_MODEL = 2048
CONV_DIM = D_MODEL // 2
CONV_WIDTH = 31
RET_HEADS = 8
RET_QK_DIM = D_MODEL // 2
RET_V_DIM = D_MODEL
QK_HEAD = RET_QK_DIM // RET_HEADS
V_HEAD = RET_V_DIM // RET_HEADS
D_FF = 4 * D_MODEL
ROPE_BASE = 10000.0
EPS = 1e-6
IN_COLS = 2 * CONV_DIM + 2 * RET_QK_DIM + 2 * RET_V_DIM + 2 * D_MODEL
N_MOD = 6

RET_CHUNK = 256
RET_ROWS = 128
CONV_HIST = 32
CONV_ROWS = 128
CONV_PIECE_ROWS = 32
CONV_LANES = 128

VMEM_LIMIT = 56 * 1024 * 1024
FFN_VMEM_LIMIT = 60 * 1024 * 1024
BF16 = jnp.bfloat16
F32 = jnp.float32


def _params(sem, vmem=VMEM_LIMIT):
    return pltpu.CompilerParams(dimension_semantics=sem, vmem_limit_bytes=vmem)


def _mod_kernel(c_ref, w_ref, b_ref, o_ref):
    c = c_ref[...]
    ca = (c * jax.nn.sigmoid(c)).astype(BF16)
    o_ref[...] = jnp.dot(ca, w_ref[...].astype(BF16), preferred_element_type=F32) + b_ref[...]


def _mod(c, w_ada, b_ada):
    B, D = c.shape
    N = w_ada.shape[1]
    tn = 1024
    return pl.pallas_call(
        _mod_kernel,
        out_shape=jax.ShapeDtypeStruct((B, N), F32),
        grid=(N // tn,),
        in_specs=[
            pl.BlockSpec((B, D), lambda j: (0, 0)),
            pl.BlockSpec((D, tn), lambda j: (0, j)),
            pl.BlockSpec((1, tn), lambda j: (0, j)),
        ],
        out_specs=pl.BlockSpec((B, tn), lambda j: (0, j)),
        compiler_params=_params(("arbitrary",)),
        name="mod",
    )(c, w_ada, b_ada.reshape(1, N))


NORM_ROWS = 16
NORM_UNROLL = 8


def _row_loop(n_rows, fn):
    def body(r, carry):
        fn(pl.multiple_of(r * NORM_ROWS, NORM_ROWS))
        return carry

    lax.fori_loop(0, n_rows // NORM_ROWS, body, 0, unroll=NORM_UNROLL)


def _rms_modulate_into(x_ref, h_ref, g, shift, scale):
    gs = g * (1.0 + scale)

    def rows(r0):
        x = x_ref[pl.ds(r0, NORM_ROWS), :]
        ms = jnp.mean(x * x, axis=-1, keepdims=True)
        h_ref[pl.ds(r0, NORM_ROWS), :] = (x * lax.rsqrt(ms + EPS) * gs + shift).astype(h_ref.dtype)

    _row_loop(x_ref.shape[0], rows)


def _conv_rows(win_ref, y_ref, o_ref, w_ref, b_ref, lg, lb, zero_ref):
    SUB = 8
    R, LC = CONV_PIECE_ROWS, CONV_LANES
    zero = zero_ref[...]

    def token(v):
        return lax.bitcast_convert_type(pltpu.roll(v, 1, 1), jnp.int32) & zero

    def after(x, tok):
        if tok is None:
            return x
        return lax.bitcast_convert_type(lax.bitcast_convert_type(x, jnp.int32) | tok[0:x.shape[0]], F32)

    one_bits = jnp.int32(0x3F800000)
    tok = None
    for q in range(0, CONV_ROWS, NORM_ROWS):
        y = y_ref[q:q + NORM_ROWS, :]
        if tok is not None:
            one = lax.bitcast_convert_type(tok[0:1] | one_bits, F32)
            y = jnp.concatenate([y[:, 0:LC] * one, y[:, LC:]], axis=1)
        mu = jnp.mean(y, axis=-1, keepdims=True)
        yc = y - mu
        var = jnp.mean(yc * yc, axis=-1, keepdims=True)
        yn = yc * lax.rsqrt(var + EPS) * lg + lb
        out = yn * jax.nn.sigmoid(yn)
        tok = token(out[0:8, 0:LC])
        o_ref[q:q + NORM_ROWS, :] = out.astype(o_ref.dtype)
    tok = None
    for c in range(CONV_DIM // LC):
        ls = slice(c * LC, (c + 1) * LC)
        tails = [None] * SUB
        for r0 in range(0, CONV_ROWS, R):
            acc = None
            for b in range(SUB):
                lead = SUB if tails[b] is None else 0
                p = None
                for a8 in range(-(-CONV_WIDTH // SUB)):
                    j = SUB * a8 + b
                    if j >= CONV_WIDTH:
                        continue
                    start = r0 + CONV_HIST - lead - SUB * a8
                    k = CONV_WIDTH - 1 - j
                    w = w_ref[k:k + 1, ls]
                    if p is None:
                        w = after(w, tok)
                    term = win_ref[start:start + R + lead, ls] * w
                    p = term if p is None else p + term
                if tails[b] is not None:
                    p = jnp.concatenate([tails[b], p], axis=0)
                tails[b] = p[R:]
                if b:
                    p = pltpu.roll(p, b, 0)
                p = p[SUB:]
                acc = p if acc is None else acc + p
            tok = token(acc[0:8])
            y_ref[r0:r0 + R, ls] = acc + b_ref[:, ls]


GLU_STEP = 2
CONV_FIRST_STEP = 3
NORM_FIRST_STEP = 4


def _block_of_step(j, first_step, n_blocks):
    return jnp.clip(j - first_step, 0, n_blocks - 1)


def _inproj_kernel(*refs, tiles_per_seq, cast_steps):
    n_cast = len(cast_steps)
    x_ref, g_ref, sh_ref, sc_ref, w_ref, cw_ref, cb_ref, lg_ref, lb_ref, zero_ref = refs[:10]
    cast_in = refs[10:10 + n_cast]
    o_ref, ac_ref = refs[10 + n_cast:12 + n_cast]
    cast_out = refs[12 + n_cast:12 + 2 * n_cast]
    h_ref, stash_ref, vext_ref, win_ref, y_ref = refs[12 + 2 * n_cast:]
    i = pl.program_id(0)
    j = pl.program_id(1)
    tm = x_ref.shape[0]

    for steps in sorted(set(cast_steps)):
        @pl.when(j < steps)
        def _(steps=steps):
            for src, dst, n in zip(cast_in, cast_out, cast_steps):
                if n == steps:
                    dst[...] = src[...].astype(dst.dtype)

    @pl.when(j == 0)
    def _():
        _rms_modulate_into(x_ref, h_ref, g_ref[...], sh_ref[0], sc_ref[0])

    @pl.when(jnp.logical_and(i == 0, j == 0))
    def _():
        vext_ref[...] = jnp.zeros(vext_ref.shape, F32)
        win_ref[...] = jnp.zeros(win_ref.shape, F32)
        y_ref[...] = jnp.zeros(y_ref.shape, F32)

    @pl.when(j == GLU_STEP)
    def _():
        first = i % tiles_per_seq == 0

        @pl.when(first)
        def _():
            vext_ref[0:CONV_HIST, :] = jnp.zeros((CONV_HIST, CONV_DIM), F32)

        @pl.when(jnp.logical_not(first))
        def _():
            vext_ref[0:CONV_HIST, :] = vext_ref[tm:tm + CONV_HIST, :]

        a = stash_ref[0].astype(F32)
        gt = stash_ref[1].astype(F32)
        vext_ref[CONV_HIST:CONV_HIST + tm, :] = a * jax.nn.sigmoid(gt)

    _conv_rows(win_ref, y_ref, ac_ref, cw_ref, cb_ref, lg_ref[...], lb_ref[...], zero_ref)

    o_ref[...] = jnp.dot(h_ref[...], w_ref[...], preferred_element_type=F32).astype(o_ref.dtype)

    nxt = _block_of_step(j + 1, CONV_FIRST_STEP, tm // CONV_ROWS)
    win_ref[...] = vext_ref[pl.ds(pl.multiple_of(nxt * CONV_ROWS, CONV_ROWS), CONV_HIST + CONV_ROWS), :]

    @pl.when(j < 2)
    def _():
        stash_ref[j] = o_ref[...]


BF16_SUBLANES = 16


def _inproj(x2d, g, shift, scale, w, conv_w, conv_b, ln_g, ln_b, seq, to_cast):
    M, D = x2d.shape
    N = w.shape[1]
    C = CONV_DIM
    tm, tn = 1024, 1024
    per_b = seq // tm
    n_blocks = tm // CONV_ROWS
    n_i, n_j = M // tm, N // tn
    assert tn == C and n_j >= NORM_FIRST_STEP + n_blocks
    const = lambda i, j: (0, 0)
    cast_steps, cast_specs = [], []
    for wk in to_cast:
        steps = 1
        while steps * 2 <= n_j and wk.shape[0] % (n_i * steps * 2 * BF16_SUBLANES) == 0:
            steps *= 2
        rows = wk.shape[0] // (n_i * steps)
        assert rows % BF16_SUBLANES == 0
        cast_steps.append(steps)
        cast_specs.append(((rows, wk.shape[1]),
                           lambda i, j, steps=steps: (i * steps + jnp.minimum(j, steps - 1), 0)))
    return pl.pallas_call(
        functools.partial(_inproj_kernel, tiles_per_seq=per_b, cast_steps=tuple(cast_steps)),
        out_shape=(jax.ShapeDtypeStruct((M, N), BF16), jax.ShapeDtypeStruct((M, C), BF16),
                   *(jax.ShapeDtypeStruct(wk.shape, BF16) for wk in to_cast)),
        grid=(n_i, n_j),
        in_specs=[
            pl.BlockSpec((tm, D), lambda i, j: (i, 0)),
            pl.BlockSpec((1, D), const),
            pl.BlockSpec((1, 1, D), lambda i, j: (i // per_b, 0, 0)),
            pl.BlockSpec((1, 1, D), lambda i, j: (i // per_b, 0, 0)),
            pl.BlockSpec((D, tn), lambda i, j: (0, j)),
            pl.BlockSpec((CONV_WIDTH, C), const),
            pl.BlockSpec((1, C), const),
            pl.BlockSpec((1, C), const),
            pl.BlockSpec((1, C), const),
            pl.BlockSpec((1, CONV_LANES), const),
            *(pl.BlockSpec(shape, index) for shape, index in cast_specs),
        ],
        out_specs=(pl.BlockSpec((tm, tn), lambda i, j: (i, j)),
                   pl.BlockSpec((CONV_ROWS, C),
                                lambda i, j: (i * n_blocks + _block_of_step(j, NORM_FIRST_STEP, n_blocks), 0)),
                   *(pl.BlockSpec(shape, index) for shape, index in cast_specs)),
        scratch_shapes=[
            pltpu.VMEM((tm, D), BF16),
            pltpu.VMEM((2, tm, C), BF16),
            pltpu.VMEM((CONV_HIST + tm, C), F32),
            pltpu.VMEM((CONV_HIST + CONV_ROWS, C), F32),
            pltpu.VMEM((CONV_ROWS, C), F32),
        ],
        compiler_params=_params(("arbitrary", "arbitrary")),
        name="in_proj",
    )(x2d, g, shift, scale, w, conv_w, conv_b.reshape(1, C), ln_g.reshape(1, C), ln_b.reshape(1, C),
      jnp.zeros((1, CONV_LANES), jnp.int32), *to_cast)


def _ret_kernel(pos_ref, q_ref, k_ref, v_ref, g_ref, invf_ref, sgn_ref, dq_ref, dk_ref,
                cd_ref, gng_ref, gnb_ref, o_ref, r_ref, *, ts):
    s = pl.program_id(1)

    @pl.when(s == 0)
    def _():
        r_ref[...] = jnp.zeros(r_ref.shape, F32)

    half = QK_HEAD // 2
    pos = pos_ref[...].astype(F32)
    invf = invf_ref[...]
    low = lax.broadcasted_iota(jnp.int32, (ts // 2, QK_HEAD), 1) < half
    ang = jnp.where(low, pos[:ts // 2] * invf, pos[ts // 2:] * invf)
    c2, s2 = jnp.cos(ang), jnp.sin(ang)
    c2r, s2r = pltpu.roll(c2, half, 1), pltpu.roll(s2, half, 1)
    cosf = jnp.concatenate([jnp.where(low, c2, c2r), jnp.where(low, c2r, c2)], axis=0)
    sinf = jnp.concatenate([jnp.where(low, s2, s2r), jnp.where(low, s2r, s2)], axis=0) * sgn_ref[...]
    C, SB = RET_CHUNK, RET_ROWS
    causal = []
    for m in range(C // SB):
        row = lax.broadcasted_iota(jnp.int32, (SB, (m + 1) * SB), 0) + m * SB
        col = lax.broadcasted_iota(jnp.int32, (SB, (m + 1) * SB), 1)
        causal.append(col <= row)
    for n in range(ts // C):
        rs = slice(n * C, (n + 1) * C)
        cs_, sn_ = cosf[rs], sinf[rs]
        qds, kds, ps = [], [], []
        for h in range(RET_HEADS):
            qs = slice(h * QK_HEAD, (h + 1) * QK_HEAD)
            q = q_ref[rs, qs].astype(F32)
            k = k_ref[rs, qs].astype(F32)
            qd = ((q * cs_ + pltpu.roll(q, QK_HEAD // 2, 1) * sn_) * dq_ref[h]).astype(BF16)
            kd = ((k * cs_ + pltpu.roll(k, QK_HEAD // 2, 1) * sn_) * dk_ref[h]).astype(BF16)
            qds.append(qd)
            kds.append(kd)
            for m in range(C // SB):
                kcols = (m + 1) * SB
                sc = lax.dot_general(qd[m * SB:(m + 1) * SB], kd[:kcols], (((1,), (1,)), ((), ())),
                                     preferred_element_type=F32)
                ps.append(jnp.where(causal[m], sc, 0.0).astype(BF16))
        for h in range(RET_HEADS):
            vs = slice(h * V_HEAD, (h + 1) * V_HEAD)
            v = v_ref[rs, vs]
            Rb = r_ref[h].astype(BF16)
            gng = gng_ref[:, vs]
            gnb = gnb_ref[:, vs]
            for m in range(C // SB):
                kcols = (m + 1) * SB
                lhs = jnp.concatenate([ps[h * (C // SB) + m], qds[h][m * SB:(m + 1) * SB]], axis=1)
                rhs = jnp.concatenate([v[:kcols], Rb], axis=0)
                o = jnp.dot(lhs, rhs, preferred_element_type=F32)
                mu = jnp.mean(o, axis=-1, keepdims=True)
                oc = o - mu
                var = jnp.mean(oc * oc, axis=-1, keepdims=True)
                on = oc * lax.rsqrt(var + EPS) * gng + gnb
                orow = slice(n * C + m * SB, n * C + (m + 1) * SB)
                g = g_ref[orow, vs]
                o_ref[orow, vs] = g * jax.nn.sigmoid(g) * on.astype(o_ref.dtype)
        for h in range(RET_HEADS):
            vs = slice(h * V_HEAD, (h + 1) * V_HEAD)
            kv = lax.dot_general(kds[h], v_ref[rs, vs], (((0,), (0,)), ((), ())),
                                 preferred_element_type=F32)
            r_ref[h] = (r_ref[h] + kv) * cd_ref[h]


def _ret_tables():
    C = RET_CHUNK
    log_gamma = jnp.log(1.0 - jnp.exp2(-5.0 - jnp.arange(RET_HEADS, dtype=F32)))
    idx = jnp.arange(C, dtype=F32)
    dq = jnp.exp((idx + 1.0)[None, :] * log_gamma[:, None])
    dk = jnp.exp(-(idx + 1.0)[None, :] * log_gamma[:, None]) * (QK_HEAD ** -0.5)
    cd = jnp.exp(C * log_gamma)
    half = QK_HEAD // 2
    inv_freq = ROPE_BASE ** (-jnp.arange(0, half, dtype=F32) / half)
    invf = jnp.concatenate([inv_freq, inv_freq]).reshape(1, QK_HEAD)
    sgn = jnp.concatenate([-jnp.ones((half,), F32), jnp.ones((half,), F32)]).reshape(1, QK_HEAD)
    cd_b = jnp.broadcast_to(cd[:, None, None], (RET_HEADS, 1, V_HEAD))
    lanes = (RET_HEADS, C, QK_HEAD)
    return (invf, sgn, jnp.broadcast_to(dq[:, :, None], lanes), jnp.broadcast_to(dk[:, :, None], lanes), cd_b)


def _ret(proj, positions, gn_g, gn_b, batch, seq):
    M = proj.shape[0]
    ts = 512
    per_b = seq // ts
    C = RET_CHUNK
    invf, sgn, dq, dk, cd = _ret_tables()
    row = lambda b, s: b * per_b + s
    const2 = lambda b, s: (0, 0)
    const3 = lambda b, s: (0, 0, 0)
    return pl.pallas_call(
        functools.partial(_ret_kernel, ts=ts),
        out_shape=jax.ShapeDtypeStruct((M, RET_V_DIM), BF16),
        grid=(batch, per_b),
        in_specs=[
            pl.BlockSpec((ts, 1), lambda b, s: (row(b, s), 0)),
            pl.BlockSpec((ts, RET_QK_DIM), lambda b, s: (row(b, s), 2)),
            pl.BlockSpec((ts, RET_QK_DIM), lambda b, s: (row(b, s), 3)),
            pl.BlockSpec((ts, RET_V_DIM), lambda b, s: (row(b, s), 2)),
            pl.BlockSpec((ts, RET_V_DIM), lambda b, s: (row(b, s), 3)),
            pl.BlockSpec((1, QK_HEAD), const2),
            pl.BlockSpec((1, QK_HEAD), const2),
            pl.BlockSpec((RET_HEADS, C, QK_HEAD), const3),
            pl.BlockSpec((RET_HEADS, C, QK_HEAD), const3),
            pl.BlockSpec((RET_HEADS, 1, V_HEAD), const3),
            pl.BlockSpec((1, RET_V_DIM), const2),
            pl.BlockSpec((1, RET_V_DIM), const2),
        ],
        out_specs=pl.BlockSpec((ts, RET_V_DIM), lambda b, s: (row(b, s), 0)),
        scratch_shapes=[pltpu.VMEM((RET_HEADS, QK_HEAD, V_HEAD), F32)],
        compiler_params=_params(("parallel", "arbitrary")),
        name="retention",
    )(positions.reshape(M, 1), proj, proj, proj, proj, invf, sgn, dq, dk, cd,
      gn_g.reshape(1, RET_V_DIM), gn_b.reshape(1, RET_V_DIM))


def _mixout_kernel(ac_ref, ar_ref, gc_ref, gr_ref, x_ref, gm_ref, wc_ref, wr_ref, wo_ref, o_ref):
    yc = jnp.dot(ac_ref[...], wc_ref[...], preferred_element_type=F32)
    yr = jnp.dot(ar_ref[...], wr_ref[...], preferred_element_type=F32)
    merged = (jax.nn.sigmoid(gc_ref[...].astype(F32)) * yc
              + jax.nn.sigmoid(gr_ref[...].astype(F32)) * yr)
    out = jnp.dot(merged.astype(BF16), wo_ref[...], preferred_element_type=F32)
    o_ref[...] = x_ref[...] + gm_ref[0] * out


def _mixout(a_conv, a_ret, proj, x2d, gate_m, wc, wr, wo, seq):
    M, D = x2d.shape
    tm = 256
    per_b = seq // tm
    resident = functools.partial(pl.BlockSpec, pipeline_mode=pl.Buffered(1))
    return pl.pallas_call(
        _mixout_kernel,
        out_shape=jax.ShapeDtypeStruct((M, D), F32),
        grid=(M // tm,),
        in_specs=[
            pl.BlockSpec((tm, CONV_DIM), lambda i: (i, 0)),
            pl.BlockSpec((tm, RET_V_DIM), lambda i: (i, 0)),
            pl.BlockSpec((tm, D), lambda i: (i, 4)),
            pl.BlockSpec((tm, D), lambda i: (i, 5)),
            pl.BlockSpec((tm, D), lambda i: (i, 0)),
            pl.BlockSpec((1, 1, D), lambda i: (i // per_b, 0, 0)),
            resident((CONV_DIM, D), lambda i: (0, 0)),
            resident((RET_V_DIM, D), lambda i: (0, 0)),
            resident((D, D), lambda i: (0, 0)),
        ],
        out_specs=pl.BlockSpec((tm, D), lambda i: (i, 0)),
        compiler_params=_params(("parallel",)),
        name="mix_out",
    )(a_conv, a_ret, proj, proj, x2d, gate_m, wc, wr, wo)


def _ffn_kernel(x_ref, g_ref, sh_ref, sc_ref, gf_ref, w1_ref, w2_ref, gfin_ref, o_ref, h_ref):
    j = pl.program_id(1)

    def chunk():
        hid = jnp.dot(h_ref[...], w1_ref[...], preferred_element_type=F32)
        hid = jnp.square(jnp.maximum(hid, 0.0)).astype(BF16)
        return jnp.dot(hid, w2_ref[...], preferred_element_type=F32)

    @pl.when(j == 0)
    def _():
        _rms_modulate_into(x_ref, h_ref, g_ref[...], sh_ref[0], sc_ref[0])
        o_ref[...] = chunk()

    @pl.when(j != 0)
    def _():
        o_ref[...] += chunk()

    @pl.when(j == pl.num_programs(1) - 1)
    def _():
        gf = gf_ref[0]
        gfin = gfin_ref[...]

        def body(r, carry):
            base = r * (NORM_ROWS * NORM_UNROLL)
            rows = [pl.ds(pl.multiple_of(base + u * NORM_ROWS, NORM_ROWS), NORM_ROWS)
                    for u in range(NORM_UNROLL)]
            outs = []
            for rs in rows:
                x2 = x_ref[rs, :] + gf * o_ref[rs, :]
                ms = jnp.mean(x2 * x2, axis=-1, keepdims=True)
                outs.append(x2 * lax.rsqrt(ms + EPS) * gfin)
            for rs, out in zip(rows, outs):
                o_ref[rs, :] = out
            return carry

        lax.fori_loop(0, x_ref.shape[0] // (NORM_ROWS * NORM_UNROLL), body, 0)


def _ffn(x1, g, shift, scale, gate, w1, w2, g_final, seq):
    M, D = x1.shape
    F = w1.shape[1]
    tm, tf = 1024, 1024
    per_b = seq // tm
    bidx = lambda i, j: (i // per_b, 0, 0)
    return pl.pallas_call(
        _ffn_kernel,
        out_shape=jax.ShapeDtypeStruct((M, D), F32),
        grid=(M // tm, F // tf),
        in_specs=[
            pl.BlockSpec((tm, D), lambda i, j: (i, 0)),
            pl.BlockSpec((1, D), lambda i, j: (0, 0)),
            pl.BlockSpec((1, 1, D), bidx),
            pl.BlockSpec((1, 1, D), bidx),
            pl.BlockSpec((1, 1, D), bidx),
            pl.BlockSpec((D, tf), lambda i, j: (0, j)),
            pl.BlockSpec((tf, D), lambda i, j: (j, 0)),
            pl.BlockSpec((1, D), lambda i, j: (0, 0)),
        ],
        out_specs=pl.BlockSpec((tm, D), lambda i, j: (i, 0)),
        scratch_shapes=[pltpu.VMEM((tm, D), BF16)],
        compiler_params=_params(("parallel", "arbitrary"), vmem=FFN_VMEM_LIMIT),
        name="ffn",
    )(x1, g, shift, scale, gate, w1, w2, g_final)


def kernel(x, c, positions, w_ada, b_ada, g_norm_mix, w_in, conv_w, conv_b, conv_ln_g, conv_ln_b,
           w_conv_out, ret_gn_g, ret_gn_b, w_ret_out, w_out, g_norm_ffn, w_ff1, w_ff2, g_norm_final):
    B, S, D = x.shape
    assert w_ada.shape[0] == 1, "kernel supports DEPTH == 1"
    l = 0
    x2d = x.reshape(B * S, D)
    mod = _mod(c, w_ada[l], b_ada[l]).reshape(B, N_MOD, 1, D)
    shift_m, scale_m, gate_m, shift_f, scale_f, gate_f = (mod[:, i] for i in range(N_MOD))
    proj, a_conv, wc, wr, wo, w1, w2 = _inproj(
        x2d, g_norm_mix[l].reshape(1, D), shift_m, scale_m, w_in[l].astype(BF16),
        conv_w[l], conv_b[l], conv_ln_g[l], conv_ln_b[l], S,
        to_cast=(w_conv_out[l], w_ret_out[l], w_out[l], w_ff1[l], w_ff2[l]))
    a_ret = _ret(proj, positions, ret_gn_g[l], ret_gn_b[l], B, S)
    x1 = _mixout(a_conv, a_ret, proj, x2d, gate_m, wc, wr, wo, S)
    out = _ffn(x1, g_norm_ffn[l].reshape(1, D), shift_f, scale_f, gate_f, w1, w2,
               g_norm_final.reshape(1, D), S)
    return out.reshape(B, S, D)
```

```python
import functools

import jax
import jax.numpy as jnp
from jax import lax
from jax.experimental import pallas as pl
from jax.experimental.pallas import tpu as pltpu

D_MODEL = 2048
CONV_DIM = D_MODEL // 2
CONV_WIDTH = 31
RET_HEADS = 8
RET_QK_DIM = D_MODEL // 2
RET_V_DIM = D_MODEL
QK_HEAD = RET_QK_DIM // RET_HEADS
V_HEAD = RET_V_DIM // RET_HEADS
D_FF = 4 * D_MODEL
ROPE_BASE = 10000.0
EPS = 1e-6
IN_COLS = 2 * CONV_DIM + 2 * RET_QK_DIM + 2 * RET_V_DIM + 2 * D_MODEL
N_MOD = 6

RET_CHUNK = 256
RET_ROWS = 128
CONV_HIST = 32
CONV_ROWS = 128
CONV_PIECE_ROWS = 32
CONV_LANES = 128

VMEM_LIMIT = 56 * 1024 * 1024
FFN_VMEM_LIMIT = 60 * 1024 * 1024
BF16 = jnp.bfloat16
F32 = jnp.float32


def _params(sem, vmem=VMEM_LIMIT):
    return pltpu.CompilerParams(dimension_semantics=sem, vmem_limit_bytes=vmem)


def _mod_kernel(c_ref, w_ref, b_ref, o_ref):
    c = c_ref[...]
    ca = (c * jax.nn.sigmoid(c)).astype(BF16)
    o_ref[...] = jnp.dot(ca, w_ref[...].astype(BF16), preferred_element_type=F32) + b_ref[...]


def _mod(c, w_ada, b_ada):
    B, D = c.shape
    N = w_ada.shape[1]
    tn = 1024
    return pl.pallas_call(
        _mod_kernel,
        out_shape=jax.ShapeDtypeStruct((B, N), F32),
        grid=(N // tn,),
        in_specs=[
            pl.BlockSpec((B, D), lambda j: (0, 0)),
            pl.BlockSpec((D, tn), lambda j: (0, j)),
            pl.BlockSpec((1, tn), lambda j: (0, j)),
        ],
        out_specs=pl.BlockSpec((B, tn), lambda j: (0, j)),
        compiler_params=_params(("arbitrary",)),
        name="mod",
    )(c, w_ada, b_ada.reshape(1, N))


NORM_ROWS = 16
NORM_UNROLL = 8


def _row_loop(n_rows, fn):
    def body(r, carry):
        fn(pl.multiple_of(r * NORM_ROWS, NORM_ROWS))
        return carry

    lax.fori_loop(0, n_rows // NORM_ROWS, body, 0, unroll=NORM_UNROLL)


def _rms_modulate_into(x_ref, h_ref, g, shift, scale):
    gs = g * (1.0 + scale)

    def rows(r0):
        x = x_ref[pl.ds(r0, NORM_ROWS), :]
        ms = jnp.mean(x * x, axis=-1, keepdims=True)
        h_ref[pl.ds(r0, NORM_ROWS), :] = (x * lax.rsqrt(ms + EPS) * gs + shift).astype(h_ref.dtype)

    _row_loop(x_ref.shape[0], rows)


def _conv_rows(win_ref, y_ref, w_ref, b_ref, zero_ref):
    SUB = 8
    R, LC = CONV_PIECE_ROWS, CONV_LANES
    zero = zero_ref[...]

    def token(v):
        return lax.bitcast_convert_type(pltpu.roll(v, 1, 1), jnp.int32) & zero

    def after(x, tok):
        if tok is None:
            return x
        return lax.bitcast_convert_type(lax.bitcast_convert_type(x, jnp.int32) | tok[0:x.shape[0]], F32)

    tok = None
    for c in range(CONV_DIM // LC):
        ls = slice(c * LC, (c + 1) * LC)
        tails = [None] * SUB
        for r0 in range(0, CONV_ROWS, R):
            acc = None
            for b in range(SUB):
                lead = SUB if tails[b] is None else 0
                p = None
                for a8 in range(-(-CONV_WIDTH // SUB)):
                    j = SUB * a8 + b
                    if j >= CONV_WIDTH:
                        continue
                    start = r0 + CONV_HIST - lead - SUB * a8
                    k = CONV_WIDTH - 1 - j
                    w = w_ref[k:k + 1, ls]
                    if p is None:
                        w = after(w, tok)
                    term = win_ref[start:start + R + lead, ls] * w
                    p = term if p is None else p + term
                if tails[b] is not None:
                    p = jnp.concatenate([tails[b], p], axis=0)
                tails[b] = p[R:]
                if b:
                    p = pltpu.roll(p, b, 0)
                p = p[SUB:]
                acc = p if acc is None else acc + p
            tok = token(acc[0:8])
            y_ref[r0:r0 + R, ls] = acc + b_ref[:, ls]


GLU_STEP = 2
CONV_FIRST_STEP = 3


def _block_of_step(j, first_step, n_blocks):
    return jnp.clip(j - first_step, 0, n_blocks - 1)


def _inproj_kernel(*refs, tiles_per_seq, cast_steps):
    n_cast = len(cast_steps)
    x_ref, g_ref, sh_ref, sc_ref, w_ref, cw_ref, cb_ref, zero_ref = refs[:8]
    cast_in = refs[8:8 + n_cast]
    o_ref, y_ref = refs[8 + n_cast:10 + n_cast]
    cast_out = refs[10 + n_cast:10 + 2 * n_cast]
    h_ref, stash_ref, vext_ref, win_ref = refs[10 + 2 * n_cast:]
    i = pl.program_id(0)
    j = pl.program_id(1)
    tm = x_ref.shape[0]

    for steps in sorted(set(cast_steps)):
        @pl.when(j < steps)
        def _(steps=steps):
            for src, dst, n in zip(cast_in, cast_out, cast_steps):
                if n == steps:
                    dst[...] = src[...].astype(dst.dtype)

    @pl.when(j == 0)
    def _():
        _rms_modulate_into(x_ref, h_ref, g_ref[...], sh_ref[0], sc_ref[0])

    @pl.when(jnp.logical_and(i == 0, j == 0))
    def _():
        vext_ref[...] = jnp.zeros(vext_ref.shape, F32)
        win_ref[...] = jnp.zeros(win_ref.shape, F32)

    @pl.when(j == GLU_STEP)
    def _():
        first = i % tiles_per_seq == 0

        @pl.when(first)
        def _():
            vext_ref[0:CONV_HIST, :] = jnp.zeros((CONV_HIST, CONV_DIM), F32)

        @pl.when(jnp.logical_not(first))
        def _():
            vext_ref[0:CONV_HIST, :] = vext_ref[tm:tm + CONV_HIST, :]

        a = stash_ref[0].astype(F32)
        gt = stash_ref[1].astype(F32)
        vext_ref[CONV_HIST:CONV_HIST + tm, :] = a * jax.nn.sigmoid(gt)

    _conv_rows(win_ref, y_ref, cw_ref, cb_ref, zero_ref)

    o_ref[...] = jnp.dot(h_ref[...], w_ref[...], preferred_element_type=F32).astype(o_ref.dtype)

    nxt = _block_of_step(j + 1, CONV_FIRST_STEP, tm // CONV_ROWS)
    win_ref[...] = vext_ref[pl.ds(pl.multiple_of(nxt * CONV_ROWS, CONV_ROWS), CONV_HIST + CONV_ROWS), :]

    @pl.when(j < 2)
    def _():
        stash_ref[j] = o_ref[...]


BF16_SUBLANES = 16


def _inproj(x2d, g, shift, scale, w, conv_w, conv_b, seq, to_cast):
    M, D = x2d.shape
    N = w.shape[1]
    C = CONV_DIM
    tm, tn = 1024, 1024
    per_b = seq // tm
    n_blocks = tm // CONV_ROWS
    n_i, n_j = M // tm, N // tn
    assert tn == C and n_j >= CONV_FIRST_STEP + n_blocks
    const = lambda i, j: (0, 0)
    cast_steps, cast_specs = [], []
    for wk in to_cast:
        steps = 1
        while steps * 2 <= n_j and wk.shape[0] % (n_i * steps * 2 * BF16_SUBLANES) == 0:
            steps *= 2
        rows = wk.shape[0] // (n_i * steps)
        assert rows % BF16_SUBLANES == 0
        cast_steps.append(steps)
        cast_specs.append(((rows, wk.shape[1]),
                           lambda i, j, steps=steps: (i * steps + jnp.minimum(j, steps - 1), 0)))
    return pl.pallas_call(
        functools.partial(_inproj_kernel, tiles_per_seq=per_b, cast_steps=tuple(cast_steps)),
        out_shape=(jax.ShapeDtypeStruct((M, N), BF16), jax.ShapeDtypeStruct((M, C), F32),
                   *(jax.ShapeDtypeStruct(wk.shape, BF16) for wk in to_cast)),
        grid=(n_i, n_j),
        in_specs=[
            pl.BlockSpec((tm, D), lambda i, j: (i, 0)),
            pl.BlockSpec((1, D), const),
            pl.BlockSpec((1, 1, D), lambda i, j: (i // per_b, 0, 0)),
            pl.BlockSpec((1, 1, D), lambda i, j: (i // per_b, 0, 0)),
            pl.BlockSpec((D, tn), lambda i, j: (0, j)),
            pl.BlockSpec((CONV_WIDTH, C), const),
            pl.BlockSpec((1, C), const),
            pl.BlockSpec((1, CONV_LANES), const),
            *(pl.BlockSpec(shape, index) for shape, index in cast_specs),
        ],
        out_specs=(pl.BlockSpec((tm, tn), lambda i, j: (i, j)),
                   pl.BlockSpec((CONV_ROWS, C),
                                lambda i, j: (i * n_blocks + _block_of_step(j, CONV_FIRST_STEP, n_blocks), 0)),
                   *(pl.BlockSpec(shape, index) for shape, index in cast_specs)),
        scratch_shapes=[
            pltpu.VMEM((tm, D), BF16),
            pltpu.VMEM((2, tm, C), BF16),
            pltpu.VMEM((CONV_HIST + tm, C), F32),
            pltpu.VMEM((CONV_HIST + CONV_ROWS, C), F32),
        ],
        compiler_params=_params(("arbitrary", "arbitrary")),
        name="in_proj",
    )(x2d, g, shift, scale, w, conv_w, conv_b.reshape(1, C), jnp.zeros((1, CONV_LANES), jnp.int32), *to_cast)


def _ret_kernel(pos_ref, q_ref, k_ref, v_ref, g_ref, invf_ref, sgn_ref, dq_ref, dk_ref,
                cd_ref, gng_ref, gnb_ref, o_ref, r_ref, *, ts):
    s = pl.program_id(1)

    @pl.when(s == 0)
    def _():
        r_ref[...] = jnp.zeros(r_ref.shape, F32)

    half = QK_HEAD // 2
    pos = pos_ref[...].astype(F32)
    invf = invf_ref[...]
    low = lax.broadcasted_iota(jnp.int32, (ts // 2, QK_HEAD), 1) < half
    ang = jnp.where(low, pos[:ts // 2] * invf, pos[ts // 2:] * invf)
    c2, s2 = jnp.cos(ang), jnp.sin(ang)
    c2r, s2r = pltpu.roll(c2, half, 1), pltpu.roll(s2, half, 1)
    cosf = jnp.concatenate([jnp.where(low, c2, c2r), jnp.where(low, c2r, c2)], axis=0)
    sinf = jnp.concatenate([jnp.where(low, s2, s2r), jnp.where(low, s2r, s2)], axis=0) * sgn_ref[...]
    C, SB = RET_CHUNK, RET_ROWS
    causal = []
    for m in range(C // SB):
        row = lax.broadcasted_iota(jnp.int32, (SB, (m + 1) * SB), 0) + m * SB
        col = lax.broadcasted_iota(jnp.int32, (SB, (m + 1) * SB), 1)
        causal.append(col <= row)
    for n in range(ts // C):
        rs = slice(n * C, (n + 1) * C)
        cs_, sn_ = cosf[rs], sinf[rs]
        qds, kds, ps = [], [], []
        for h in range(RET_HEADS):
            qs = slice(h * QK_HEAD, (h + 1) * QK_HEAD)
            q = q_ref[rs, qs].astype(F32)
            k = k_ref[rs, qs].astype(F32)
            qd = ((q * cs_ + pltpu.roll(q, QK_HEAD // 2, 1) * sn_) * dq_ref[h]).astype(BF16)
            kd = ((k * cs_ + pltpu.roll(k, QK_HEAD // 2, 1) * sn_) * dk_ref[h]).astype(BF16)
            qds.append(qd)
            kds.append(kd)
            for m in range(C // SB):
                kcols = (m + 1) * SB
                sc = lax.dot_general(qd[m * SB:(m + 1) * SB], kd[:kcols], (((1,), (1,)), ((), ())),
                                     preferred_element_type=F32)
                ps.append(jnp.where(causal[m], sc, 0.0).astype(BF16))
        for h in range(RET_HEADS):
            vs = slice(h * V_HEAD, (h + 1) * V_HEAD)
            v = v_ref[rs, vs]
            Rb = r_ref[h].astype(BF16)
            gng = gng_ref[:, vs]
            gnb = gnb_ref[:, vs]
            for m in range(C // SB):
                kcols = (m + 1) * SB
                lhs = jnp.concatenate([ps[h * (C // SB) + m], qds[h][m * SB:(m + 1) * SB]], axis=1)
                rhs = jnp.concatenate([v[:kcols], Rb], axis=0)
                o = jnp.dot(lhs, rhs, preferred_element_type=F32)
                mu = jnp.mean(o, axis=-1, keepdims=True)
                oc = o - mu
                var = jnp.mean(oc * oc, axis=-1, keepdims=True)
                on = oc * lax.rsqrt(var + EPS) * gng + gnb
                orow = slice(n * C + m * SB, n * C + (m + 1) * SB)
                g = g_ref[orow, vs]
                o_ref[orow, vs] = g * jax.nn.sigmoid(g) * on.astype(o_ref.dtype)
        for h in range(RET_HEADS):
            vs = slice(h * V_HEAD, (h + 1) * V_HEAD)
            kv = lax.dot_general(kds[h], v_ref[rs, vs], (((0,), (0,)), ((), ())),
                                 preferred_element_type=F32)
            r_ref[h] = (r_ref[h] + kv) * cd_ref[h]


def _ret_tables():
    C = RET_CHUNK
    log_gamma = jnp.log(1.0 - jnp.exp2(-5.0 - jnp.arange(RET_HEADS, dtype=F32)))
    idx = jnp.arange(C, dtype=F32)
    dq = jnp.exp((idx + 1.0)[None, :] * log_gamma[:, None])
    dk = jnp.exp(-(idx + 1.0)[None, :] * log_gamma[:, None]) * (QK_HEAD ** -0.5)
    cd = jnp.exp(C * log_gamma)
    half = QK_HEAD // 2
    inv_freq = ROPE_BASE ** (-jnp.arange(0, half, dtype=F32) / half)
    invf = jnp.concatenate([inv_freq, inv_freq]).reshape(1, QK_HEAD)
    sgn = jnp.concatenate([-jnp.ones((half,), F32), jnp.ones((half,), F32)]).reshape(1, QK_HEAD)
    cd_b = jnp.broadcast_to(cd[:, None, None], (RET_HEADS, 1, V_HEAD))
    lanes = (RET_HEADS, C, QK_HEAD)
    return (invf, sgn, jnp.broadcast_to(dq[:, :, None], lanes), jnp.broadcast_to(dk[:, :, None], lanes), cd_b)


def _ret(proj, positions, gn_g, gn_b, batch, seq):
    M = proj.shape[0]
    ts = 512
    per_b = seq // ts
    C = RET_CHUNK
    invf, sgn, dq, dk, cd = _ret_tables()
    row = lambda b, s: b * per_b + s
    const2 = lambda b, s: (0, 0)
    const3 = lambda b, s: (0, 0, 0)
    return pl.pallas_call(
        functools.partial(_ret_kernel, ts=ts),
        out_shape=jax.ShapeDtypeStruct((M, RET_V_DIM), BF16),
        grid=(batch, per_b),
        in_specs=[
            pl.BlockSpec((ts, 1), lambda b, s: (row(b, s), 0)),
            pl.BlockSpec((ts, RET_QK_DIM), lambda b, s: (row(b, s), 2)),
            pl.BlockSpec((ts, RET_QK_DIM), lambda b, s: (row(b, s), 3)),
            pl.BlockSpec((ts, RET_V_DIM), lambda b, s: (row(b, s), 2)),
            pl.BlockSpec((ts, RET_V_DIM), lambda b, s: (row(b, s), 3)),
            pl.BlockSpec((1, QK_HEAD), const2),
            pl.BlockSpec((1, QK_HEAD), const2),
            pl.BlockSpec((RET_HEADS, C, QK_HEAD), const3),
            pl.BlockSpec((RET_HEADS, C, QK_HEAD), const3),
            pl.BlockSpec((RET_HEADS, 1, V_HEAD), const3),
            pl.BlockSpec((1, RET_V_DIM), const2),
            pl.BlockSpec((1, RET_V_DIM), const2),
        ],
        out_specs=pl.BlockSpec((ts, RET_V_DIM), lambda b, s: (row(b, s), 0)),
        scratch_shapes=[pltpu.VMEM((RET_HEADS, QK_HEAD, V_HEAD), F32)],
        compiler_params=_params(("parallel", "arbitrary")),
        name="retention",
    )(positions.reshape(M, 1), proj, proj, proj, proj, invf, sgn, dq, dk, cd,
      gn_g.reshape(1, RET_V_DIM), gn_b.reshape(1, RET_V_DIM))


def _mixout_kernel(yconv_ref, ar_ref, gc_ref, gr_ref, x_ref, gm_ref, lg_ref, lb_ref, wc_ref, wr_ref, wo_ref,
                   o_ref):
    yr = jnp.dot(ar_ref[...], wr_ref[...], preferred_element_type=F32)
    lg = lg_ref[...]
    lb = lb_ref[...]
    groups = []
    for q in range(0, yconv_ref.shape[0], NORM_ROWS):
        y = yconv_ref[q:q + NORM_ROWS, :]
        mu = jnp.mean(y, axis=-1, keepdims=True)
        yc = y - mu
        var = jnp.mean(yc * yc, axis=-1, keepdims=True)
        yn = yc * lax.rsqrt(var + EPS) * lg + lb
        groups.append((yn * jax.nn.sigmoid(yn)).astype(BF16))
    a_conv = jnp.concatenate(groups, axis=0)
    yc = jnp.dot(a_conv, wc_ref[...], preferred_element_type=F32)
    merged = (jax.nn.sigmoid(gc_ref[...].astype(F32)) * yc
              + jax.nn.sigmoid(gr_ref[...].astype(F32)) * yr)
    out = jnp.dot(merged.astype(BF16), wo_ref[...], preferred_element_type=F32)
    o_ref[...] = x_ref[...] + gm_ref[0] * out


def _mixout(y_conv, a_ret, proj, x2d, gate_m, ln_g, ln_b, wc, wr, wo, seq):
    M, D = x2d.shape
    tm = 256
    per_b = seq // tm
    resident = functools.partial(pl.BlockSpec, pipeline_mode=pl.Buffered(1))
    const = lambda i: (0, 0)
    return pl.pallas_call(
        _mixout_kernel,
        out_shape=jax.ShapeDtypeStruct((M, D), F32),
        grid=(M // tm,),
        in_specs=[
            pl.BlockSpec((tm, CONV_DIM), lambda i: (i, 0)),
            pl.BlockSpec((tm, RET_V_DIM), lambda i: (i, 0)),
            pl.BlockSpec((tm, D), lambda i: (i, 4)),
            pl.BlockSpec((tm, D), lambda i: (i, 5)),
            pl.BlockSpec((tm, D), lambda i: (i, 0)),
            pl.BlockSpec((1, 1, D), lambda i: (i // per_b, 0, 0)),
            pl.BlockSpec((1, CONV_DIM), const),
            pl.BlockSpec((1, CONV_DIM), const),
            resident((CONV_DIM, D), lambda i: (0, 0)),
            resident((RET_V_DIM, D), lambda i: (0, 0)),
            resident((D, D), lambda i: (0, 0)),
        ],
        out_specs=pl.BlockSpec((tm, D), lambda i: (i, 0)),
        compiler_params=_params(("parallel",)),
        name="mix_out",
    )(y_conv, a_ret, proj, proj, x2d, gate_m, ln_g.reshape(1, CONV_DIM), ln_b.reshape(1, CONV_DIM), wc, wr, wo)


def _ffn_kernel(x_ref, g_ref, sh_ref, sc_ref, gf_ref, w1_ref, w2_ref, gfin_ref, o_ref, h_ref):
    j = pl.program_id(1)

    def chunk():
        hid = jnp.dot(h_ref[...], w1_ref[...], preferred_element_type=F32)
        hid = jnp.square(jnp.maximum(hid, 0.0)).astype(BF16)
        return jnp.dot(hid, w2_ref[...], preferred_element_type=F32)

    @pl.when(j == 0)
    def _():
        _rms_modulate_into(x_ref, h_ref, g_ref[...], sh_ref[0], sc_ref[0])
        o_ref[...] = chunk()

    @pl.when(j != 0)
    def _():
        o_ref[...] += chunk()

    @pl.when(j == pl.num_programs(1) - 1)
    def _():
        gf = gf_ref[0]
        gfin = gfin_ref[...]

        def body(r, carry):
            base = r * (NORM_ROWS * NORM_UNROLL)
            rows = [pl.ds(pl.multiple_of(base + u * NORM_ROWS, NORM_ROWS), NORM_ROWS)
                    for u in range(NORM_UNROLL)]
            outs = []
            for rs in rows:
                x2 = x_ref[rs, :] + gf * o_ref[rs, :]
                ms = jnp.mean(x2 * x2, axis=-1, keepdims=True)
                outs.append(x2 * lax.rsqrt(ms + EPS) * gfin)
            for rs, out in zip(rows, outs):
                o_ref[rs, :] = out
            return carry

        lax.fori_loop(0, x_ref.shape[0] // (NORM_ROWS * NORM_UNROLL), body, 0)


def _ffn(x1, g, shift, scale, gate, w1, w2, g_final, seq):
    M, D = x1.shape
    F = w1.shape[1]
    tm, tf = 1024, 1024
    per_b = seq // tm
    bidx = lambda i, j: (i // per_b, 0, 0)
    return pl.pallas_call(
        _ffn_kernel,
        out_shape=jax.ShapeDtypeStruct((M, D), F32),
        grid=(M // tm, F // tf),
        in_specs=[
            pl.BlockSpec((tm, D), lambda i, j: (i, 0)),
            pl.BlockSpec((1, D), lambda i, j: (0, 0)),
            pl.BlockSpec((1, 1, D), bidx),
            pl.BlockSpec((1, 1, D), bidx),
            pl.BlockSpec((1, 1, D), bidx),
            pl.BlockSpec((D, tf), lambda i, j: (0, j)),
            pl.BlockSpec((tf, D), lambda i, j: (j, 0)),
            pl.BlockSpec((1, D), lambda i, j: (0, 0)),
        ],
        out_specs=pl.BlockSpec((tm, D), lambda i, j: (i, 0)),
        scratch_shapes=[pltpu.VMEM((tm, D), BF16)],
        compiler_params=_params(("parallel", "arbitrary"), vmem=FFN_VMEM_LIMIT),
        name="ffn",
    )(x1, g, shift, scale, gate, w1, w2, g_final)


def kernel(x, c, positions, w_ada, b_ada, g_norm_mix, w_in, conv_w, conv_b, conv_ln_g, conv_ln_b,
           w_conv_out, ret_gn_g, ret_gn_b, w_ret_out, w_out, g_norm_ffn, w_ff1, w_ff2, g_norm_final):
    B, S, D = x.shape
    assert w_ada.shape[0] == 1, "kernel supports DEPTH == 1"
    l = 0
    x2d = x.reshape(B * S, D)
    mod = _mod(c, w_ada[l], b_ada[l]).reshape(B, N_MOD, 1, D)
    shift_m, scale_m, gate_m, shift_f, scale_f, gate_f = (mod[:, i] for i in range(N_MOD))
    proj, y_conv, wc, wr, wo, w1, w2 = _inproj(
        x2d, g_norm_mix[l].reshape(1, D), shift_m, scale_m, w_in[l].astype(BF16),
        conv_w[l], conv_b[l], S,
        to_cast=(w_conv_out[l], w_ret_out[l], w_out[l], w_ff1[l], w_ff2[l]))
    a_ret = _ret(proj, positions, ret_gn_g[l], ret_gn_b[l], B, S)
    x1 = _mixout(y_conv, a_ret, proj, x2d, gate_m, conv_ln_g[l], conv_ln_b[l], wc, wr, wo, S)
    out = _ffn(x1, g_norm_ffn[l].reshape(1, D), shift_f, scale_f, gate_f, w1, w2,
               g_norm_final.reshape(1, D), S)
    return out.reshape(B, S, D)
```

```python
import functools

import jax
import jax.numpy as jnp
from jax import lax
from jax.experimental import pallas as pl
from jax.experimental.pallas import tpu as pltpu

D_MODEL = 2048
CONV_DIM = D_MODEL // 2
CONV_WIDTH = 31
RET_HEADS = 8
RET_QK_DIM = D_MODEL // 2
RET_V_DIM = D_MODEL
QK_HEAD = RET_QK_DIM // RET_HEADS
V_HEAD = RET_V_DIM // RET_HEADS
D_FF = 4 * D_MODEL
ROPE_BASE = 10000.0
EPS = 1e-6
IN_COLS = 2 * CONV_DIM + 2 * RET_QK_DIM + 2 * RET_V_DIM + 2 * D_MODEL
N_MOD = 6

RET_CHUNK = 256
RET_ROWS = 128
CONV_HIST = 32
CONV_ROWS = 128
CONV_PIECE_ROWS = 32
CONV_LANES = 128

VMEM_LIMIT = 56 * 1024 * 1024
FFN_VMEM_LIMIT = 60 * 1024 * 1024
BF16 = jnp.bfloat16
F32 = jnp.float32


def _params(sem, vmem=VMEM_LIMIT):
    return pltpu.CompilerParams(dimension_semantics=sem, vmem_limit_bytes=vmem)


def _mod_kernel(c_ref, w_ref, b_ref, o_ref):
    c = c_ref[...]
    ca = (c * jax.nn.sigmoid(c)).astype(BF16)
    o_ref[...] = jnp.dot(ca, w_ref[...].astype(BF16), preferred_element_type=F32) + b_ref[...]


def _mod(c, w_ada, b_ada):
    B, D = c.shape
    N = w_ada.shape[1]
    tn = 2048
    return pl.pallas_call(
        _mod_kernel,
        out_shape=jax.ShapeDtypeStruct((B, N), F32),
        grid=(N // tn,),
        in_specs=[
            pl.BlockSpec((B, D), lambda j: (0, 0)),
            pl.BlockSpec((D, tn), lambda j: (0, j)),
            pl.BlockSpec((1, tn), lambda j: (0, j)),
        ],
        out_specs=pl.BlockSpec((B, tn), lambda j: (0, j)),
        compiler_params=_params(("arbitrary",)),
        name="mod",
    )(c, w_ada, b_ada.reshape(1, N))


NORM_ROWS = 16
NORM_UNROLL = 8


def _row_loop(n_rows, fn):
    def body(r, carry):
        fn(pl.multiple_of(r * NORM_ROWS, NORM_ROWS))
        return carry

    lax.fori_loop(0, n_rows // NORM_ROWS, body, 0, unroll=NORM_UNROLL)


def _rms_modulate_into(x_ref, h_ref, g, shift, scale):
    gs = g * (1.0 + scale)

    def rows(r0):
        x = x_ref[pl.ds(r0, NORM_ROWS), :]
        ms = jnp.mean(x * x, axis=-1, keepdims=True)
        h_ref[pl.ds(r0, NORM_ROWS), :] = (x * lax.rsqrt(ms + EPS) * gs + shift).astype(h_ref.dtype)

    _row_loop(x_ref.shape[0], rows)


def _conv_rows(win_ref, y_ref, w_ref, b_ref, zero_ref):
    SUB = 8
    R, LC = CONV_PIECE_ROWS, CONV_LANES
    zero = zero_ref[...]

    def token(v):
        return lax.bitcast_convert_type(pltpu.roll(v, 1, 1), jnp.int32) & zero

    def after(x, tok):
        if tok is None:
            return x
        return lax.bitcast_convert_type(lax.bitcast_convert_type(x, jnp.int32) | tok[0:x.shape[0]], F32)

    tok = None
    for c in range(CONV_DIM // LC):
        ls = slice(c * LC, (c + 1) * LC)
        tails = [None] * SUB
        for r0 in range(0, CONV_ROWS, R):
            acc = None
            for b in range(SUB):
                lead = SUB if tails[b] is None else 0
                p = None
                for a8 in range(-(-CONV_WIDTH // SUB)):
                    j = SUB * a8 + b
                    if j >= CONV_WIDTH:
                        continue
                    start = r0 + CONV_HIST - lead - SUB * a8
                    k = CONV_WIDTH - 1 - j
                    w = w_ref[k:k + 1, ls]
                    if p is None:
                        w = after(w, tok)
                    term = win_ref[start:start + R + lead, ls] * w
                    p = term if p is None else p + term
                if tails[b] is not None:
                    p = jnp.concatenate([tails[b], p], axis=0)
                tails[b] = p[R:]
                if b:
                    p = pltpu.roll(p, b, 0)
                p = p[SUB:]
                acc = p if acc is None else acc + p
            tok = token(acc[0:8])
            y_ref[r0:r0 + R, ls] = acc + b_ref[:, ls]


GLU_STEP = 2
CONV_FIRST_STEP = 3


def _block_of_step(j, first_step, n_blocks):
    return jnp.clip(j - first_step, 0, n_blocks - 1)


def _inproj_kernel(*refs, tiles_per_seq, cast_steps):
    n_cast = len(cast_steps)
    x_ref, g_ref, sh_ref, sc_ref, w_ref, cw_ref, cb_ref, zero_ref = refs[:8]
    cast_in = refs[8:8 + n_cast]
    o_ref, y_ref = refs[8 + n_cast:10 + n_cast]
    cast_out = refs[10 + n_cast:10 + 2 * n_cast]
    h_ref, stash_ref, vext_ref, win_ref = refs[10 + 2 * n_cast:]
    i = pl.program_id(0)
    j = pl.program_id(1)
    tm = x_ref.shape[0]

    for steps in sorted(set(cast_steps)):
        @pl.when(j < steps)
        def _(steps=steps):
            for src, dst, n in zip(cast_in, cast_out, cast_steps):
                if n == steps:
                    dst[...] = src[...].astype(dst.dtype)

    @pl.when(j == 0)
    def _():
        _rms_modulate_into(x_ref, h_ref, g_ref[...], sh_ref[0], sc_ref[0])

    @pl.when(jnp.logical_and(i == 0, j == 0))
    def _():
        vext_ref[...] = jnp.zeros(vext_ref.shape, F32)
        win_ref[...] = jnp.zeros(win_ref.shape, F32)

    @pl.when(j == GLU_STEP)
    def _():
        first = i % tiles_per_seq == 0

        @pl.when(first)
        def _():
            vext_ref[0:CONV_HIST, :] = jnp.zeros((CONV_HIST, CONV_DIM), F32)

        @pl.when(jnp.logical_not(first))
        def _():
            vext_ref[0:CONV_HIST, :] = vext_ref[tm:tm + CONV_HIST, :]

        a = stash_ref[0].astype(F32)
        gt = stash_ref[1].astype(F32)
        vext_ref[CONV_HIST:CONV_HIST + tm, :] = a * (0.5 * jnp.tanh(0.5 * gt) + 0.5)

    _conv_rows(win_ref, y_ref, cw_ref, cb_ref, zero_ref)

    o_ref[...] = jnp.dot(h_ref[...], w_ref[...], preferred_element_type=F32).astype(o_ref.dtype)

    nxt = _block_of_step(j + 1, CONV_FIRST_STEP, tm // CONV_ROWS)
    win_ref[...] = vext_ref[pl.ds(pl.multiple_of(nxt * CONV_ROWS, CONV_ROWS), CONV_HIST + CONV_ROWS), :]

    @pl.when(j < 2)
    def _():
        stash_ref[j] = o_ref[...]


BF16_SUBLANES = 16


def _inproj(x2d, g, shift, scale, w, conv_w, conv_b, seq, to_cast):
    M, D = x2d.shape
    N = w.shape[1]
    C = CONV_DIM
    tm, tn = 1024, 1024
    per_b = seq // tm
    n_blocks = tm // CONV_ROWS
    n_i, n_j = M // tm, N // tn
    assert tn == C and n_j >= CONV_FIRST_STEP + n_blocks
    const = lambda i, j: (0, 0)
    cast_steps, cast_specs = [], []
    for wk in to_cast:
        steps = 1
        while steps * 2 <= n_j and wk.shape[0] % (n_i * steps * 2 * BF16_SUBLANES) == 0:
            steps *= 2
        rows = wk.shape[0] // (n_i * steps)
        assert rows % BF16_SUBLANES == 0
        cast_steps.append(steps)
        cast_specs.append(((rows, wk.shape[1]),
                           lambda i, j, steps=steps: (i * steps + jnp.minimum(j, steps - 1), 0)))
    return pl.pallas_call(
        functools.partial(_inproj_kernel, tiles_per_seq=per_b, cast_steps=tuple(cast_steps)),
        out_shape=(jax.ShapeDtypeStruct((M, N), BF16), jax.ShapeDtypeStruct((M, C), F32),
                   *(jax.ShapeDtypeStruct(wk.shape, BF16) for wk in to_cast)),
        grid=(n_i, n_j),
        in_specs=[
            pl.BlockSpec((tm, D), lambda i, j: (i, 0)),
            pl.BlockSpec((1, D), const),
            pl.BlockSpec((1, 1, D), lambda i, j: (i // per_b, 0, 0)),
            pl.BlockSpec((1, 1, D), lambda i, j: (i // per_b, 0, 0)),
            pl.BlockSpec((D, tn), lambda i, j: (0, j)),
            pl.BlockSpec((CONV_WIDTH, C), const),
            pl.BlockSpec((1, C), const),
            pl.BlockSpec((1, CONV_LANES), const),
            *(pl.BlockSpec(shape, index) for shape, index in cast_specs),
        ],
        out_specs=(pl.BlockSpec((tm, tn), lambda i, j: (i, j)),
                   pl.BlockSpec((CONV_ROWS, C),
                                lambda i, j: (i * n_blocks + _block_of_step(j, CONV_FIRST_STEP, n_blocks), 0)),
                   *(pl.BlockSpec(shape, index) for shape, index in cast_specs)),
        scratch_shapes=[
            pltpu.VMEM((tm, D), BF16),
            pltpu.VMEM((2, tm, C), BF16),
            pltpu.VMEM((CONV_HIST + tm, C), F32),
            pltpu.VMEM((CONV_HIST + CONV_ROWS, C), F32),
        ],
        compiler_params=_params(("arbitrary", "arbitrary")),
        name="in_proj",
    )(x2d, g, shift, scale, w, conv_w, conv_b.reshape(1, C), jnp.zeros((1, CONV_LANES), jnp.int32), *to_cast)


def _ret_kernel(pos_ref, q_ref, k_ref, v_ref, g_ref, invf_ref, sgn_ref, dq_ref, dk_ref,
                cd_ref, gng_ref, gnb_ref, o_ref, r_ref, *, ts):
    s = pl.program_id(1)

    @pl.when(s == 0)
    def _():
        r_ref[...] = jnp.zeros(r_ref.shape, F32)

    half = QK_HEAD // 2
    pos = pos_ref[...].astype(F32)
    invf = invf_ref[...]
    low = lax.broadcasted_iota(jnp.int32, (ts // 2, QK_HEAD), 1) < half
    ang = jnp.where(low, pos[:ts // 2] * invf, pos[ts // 2:] * invf)
    c2, s2 = jnp.cos(ang), jnp.sin(ang)
    c2r, s2r = pltpu.roll(c2, half, 1), pltpu.roll(s2, half, 1)
    cosf = jnp.concatenate([jnp.where(low, c2, c2r), jnp.where(low, c2r, c2)], axis=0)
    sinf = jnp.concatenate([jnp.where(low, s2, s2r), jnp.where(low, s2r, s2)], axis=0) * sgn_ref[...]
    C, SB = RET_CHUNK, RET_ROWS
    causal = []
    for m in range(C // SB):
        row = lax.broadcasted_iota(jnp.int32, (SB, (m + 1) * SB), 0) + m * SB
        col = lax.broadcasted_iota(jnp.int32, (SB, (m + 1) * SB), 1)
        causal.append(col <= row)
    def scores(n):
        rs = slice(n * C, (n + 1) * C)
        cs_, sn_ = cosf[rs], sinf[rs]
        qds, kds, ps = [], [], []
        for h in range(RET_HEADS):
            qs = slice(h * QK_HEAD, (h + 1) * QK_HEAD)
            q = q_ref[rs, qs].astype(F32)
            k = k_ref[rs, qs].astype(F32)
            qd = ((q * cs_ + pltpu.roll(q, QK_HEAD // 2, 1) * sn_) * dq_ref[h]).astype(BF16)
            kd = ((k * cs_ + pltpu.roll(k, QK_HEAD // 2, 1) * sn_) * dk_ref[h]).astype(BF16)
            qds.append(qd)
            kds.append(kd)
            for m in range(C // SB):
                kcols = (m + 1) * SB
                sc = lax.dot_general(qd[m * SB:(m + 1) * SB], kd[:kcols], (((1,), (1,)), ((), ())),
                                     preferred_element_type=F32)
                ps.append(jnp.where(causal[m], sc, 0.0).astype(BF16))
        return qds, kds, ps

    def outputs_and_state(n, qds, kds, ps):
        rs = slice(n * C, (n + 1) * C)
        for h in range(RET_HEADS):
            vs = slice(h * V_HEAD, (h + 1) * V_HEAD)
            v = v_ref[rs, vs]
            Rb = r_ref[h].astype(BF16)
            gng = gng_ref[:, vs]
            gnb = gnb_ref[:, vs]
            for m in range(C // SB):
                kcols = (m + 1) * SB
                lhs = jnp.concatenate([ps[h * (C // SB) + m], qds[h][m * SB:(m + 1) * SB]], axis=1)
                rhs = jnp.concatenate([v[:kcols], Rb], axis=0)
                o = jnp.dot(lhs, rhs, preferred_element_type=F32)
                mu = jnp.mean(o, axis=-1, keepdims=True)
                oc = o - mu
                var = jnp.mean(oc * oc, axis=-1, keepdims=True)
                on = oc * lax.rsqrt(var + EPS) * gng + gnb
                orow = slice(n * C + m * SB, n * C + (m + 1) * SB)
                g = g_ref[orow, vs]
                o_ref[orow, vs] = g * jax.nn.sigmoid(g) * on.astype(o_ref.dtype)
        for h in range(RET_HEADS):
            vs = slice(h * V_HEAD, (h + 1) * V_HEAD)
            kv = lax.dot_general(kds[h], v_ref[rs, vs], (((0,), (0,)), ((), ())),
                                 preferred_element_type=F32)
            r_ref[h] = (r_ref[h] + kv) * cd_ref[h]

    n_chunks = ts // C
    ahead = scores(0)
    for n in range(n_chunks):
        cur = ahead
        if n + 1 < n_chunks:
            ahead = scores(n + 1)
        outputs_and_state(n, *cur)


def _ret_tables():
    C = RET_CHUNK
    log_gamma = jnp.log(1.0 - jnp.exp2(-5.0 - jnp.arange(RET_HEADS, dtype=F32)))
    idx = jnp.arange(C, dtype=F32)
    dq = jnp.exp((idx + 1.0)[None, :] * log_gamma[:, None])
    dk = jnp.exp(-(idx + 1.0)[None, :] * log_gamma[:, None]) * (QK_HEAD ** -0.5)
    cd = jnp.exp(C * log_gamma)
    half = QK_HEAD // 2
    inv_freq = ROPE_BASE ** (-jnp.arange(0, half, dtype=F32) / half)
    invf = jnp.concatenate([inv_freq, inv_freq]).reshape(1, QK_HEAD)
    sgn = jnp.concatenate([-jnp.ones((half,), F32), jnp.ones((half,), F32)]).reshape(1, QK_HEAD)
    cd_b = jnp.broadcast_to(cd[:, None, None], (RET_HEADS, 1, V_HEAD))
    lanes = (RET_HEADS, C, QK_HEAD)
    return (invf, sgn, jnp.broadcast_to(dq[:, :, None], lanes), jnp.broadcast_to(dk[:, :, None], lanes), cd_b)


def _ret(proj, positions, gn_g, gn_b, batch, seq):
    M = proj.shape[0]
    ts = 512
    per_b = seq // ts
    C = RET_CHUNK
    invf, sgn, dq, dk, cd = _ret_tables()
    row = lambda b, s: b * per_b + s
    const2 = lambda b, s: (0, 0)
    const3 = lambda b, s: (0, 0, 0)
    return pl.pallas_call(
        functools.partial(_ret_kernel, ts=ts),
        out_shape=jax.ShapeDtypeStruct((M, RET_V_DIM), BF16),
        grid=(batch, per_b),
        in_specs=[
            pl.BlockSpec((ts, 1), lambda b, s: (row(b, s), 0)),
            pl.BlockSpec((ts, RET_QK_DIM), lambda b, s: (row(b, s), 2)),
            pl.BlockSpec((ts, RET_QK_DIM), lambda b, s: (row(b, s), 3)),
            pl.BlockSpec((ts, RET_V_DIM), lambda b, s: (row(b, s), 2)),
            pl.BlockSpec((ts, RET_V_DIM), lambda b, s: (row(b, s), 3)),
            pl.BlockSpec((1, QK_HEAD), const2),
            pl.BlockSpec((1, QK_HEAD), const2),
            pl.BlockSpec((RET_HEADS, C, QK_HEAD), const3),
            pl.BlockSpec((RET_HEADS, C, QK_HEAD), const3),
            pl.BlockSpec((RET_HEADS, 1, V_HEAD), const3),
            pl.BlockSpec((1, RET_V_DIM), const2),
            pl.BlockSpec((1, RET_V_DIM), const2),
        ],
        out_specs=pl.BlockSpec((ts, RET_V_DIM), lambda b, s: (row(b, s), 0)),
        scratch_shapes=[pltpu.VMEM((RET_HEADS, QK_HEAD, V_HEAD), F32)],
        compiler_params=_params(("parallel", "arbitrary")),
        name="retention",
    )(positions.reshape(M, 1), proj, proj, proj, proj, invf, sgn, dq, dk, cd,
      gn_g.reshape(1, RET_V_DIM), gn_b.reshape(1, RET_V_DIM))


def _mixout_kernel(yconv_ref, ar_ref, gc_ref, gr_ref, x_ref, gm_ref, lg_ref, lb_ref, wc_ref, wr_ref, wo_ref,
                   o_ref):
    yr = jnp.dot(ar_ref[...], wr_ref[...], preferred_element_type=F32)
    lg = lg_ref[...]
    lb = lb_ref[...]
    groups = []
    for q in range(0, yconv_ref.shape[0], NORM_ROWS):
        y = yconv_ref[q:q + NORM_ROWS, :]
        mu = jnp.mean(y, axis=-1, keepdims=True)
        yc = y - mu
        var = jnp.mean(yc * yc, axis=-1, keepdims=True)
        yn = yc * lax.rsqrt(var + EPS) * lg + lb
        groups.append((yn * jax.nn.sigmoid(yn)).astype(BF16))
    a_conv = jnp.concatenate(groups, axis=0)
    yc = jnp.dot(a_conv, wc_ref[...], preferred_element_type=F32)
    merged = (jax.nn.sigmoid(gc_ref[...].astype(F32)) * yc
              + jax.nn.sigmoid(gr_ref[...].astype(F32)) * yr)
    out = jnp.dot(merged.astype(BF16), wo_ref[...], preferred_element_type=F32)
    o_ref[...] = x_ref[...] + gm_ref[0] * out


def _mixout(y_conv, a_ret, proj, x2d, gate_m, ln_g, ln_b, wc, wr, wo, seq):
    M, D = x2d.shape
    tm = 256
    per_b = seq // tm
    resident = functools.partial(pl.BlockSpec, pipeline_mode=pl.Buffered(1))
    const = lambda i: (0, 0)
    return pl.pallas_call(
        _mixout_kernel,
        out_shape=jax.ShapeDtypeStruct((M, D), F32),
        grid=(M // tm,),
        in_specs=[
            pl.BlockSpec((tm, CONV_DIM), lambda i: (i, 0)),
            pl.BlockSpec((tm, RET_V_DIM), lambda i: (i, 0)),
            pl.BlockSpec((tm, D), lambda i: (i, 4)),
            pl.BlockSpec((tm, D), lambda i: (i, 5)),
            pl.BlockSpec((tm, D), lambda i: (i, 0)),
            pl.BlockSpec((1, 1, D), lambda i: (i // per_b, 0, 0)),
            pl.BlockSpec((1, CONV_DIM), const),
            pl.BlockSpec((1, CONV_DIM), const),
            resident((CONV_DIM, D), lambda i: (0, 0)),
            resident((RET_V_DIM, D), lambda i: (0, 0)),
            resident((D, D), lambda i: (0, 0)),
        ],
        out_specs=pl.BlockSpec((tm, D), lambda i: (i, 0)),
        compiler_params=_params(("parallel",)),
        name="mix_out",
    )(y_conv, a_ret, proj, proj, x2d, gate_m, ln_g.reshape(1, CONV_DIM), ln_b.reshape(1, CONV_DIM), wc, wr, wo)


def _ffn_kernel(x_ref, g_ref, sh_ref, sc_ref, gf_ref, w1_ref, w2_ref, gfin_ref, o_ref, h_ref):
    j = pl.program_id(1)

    def chunk():
        hid = jnp.dot(h_ref[...], w1_ref[...], preferred_element_type=F32)
        hid = jnp.square(jnp.maximum(hid, 0.0)).astype(BF16)
        return jnp.dot(hid, w2_ref[...], preferred_element_type=F32)

    @pl.when(j == 0)
    def _():
        _rms_modulate_into(x_ref, h_ref, g_ref[...], sh_ref[0], sc_ref[0])
        o_ref[...] = chunk()

    @pl.when(j != 0)
    def _():
        o_ref[...] += chunk()

    @pl.when(j == pl.num_programs(1) - 1)
    def _():
        gf = gf_ref[0]
        gfin = gfin_ref[...]

        def body(r, carry):
            base = r * (NORM_ROWS * NORM_UNROLL)
            rows = [pl.ds(pl.multiple_of(base + u * NORM_ROWS, NORM_ROWS), NORM_ROWS)
                    for u in range(NORM_UNROLL)]
            outs = []
            for rs in rows:
                x2 = x_ref[rs, :] + gf * o_ref[rs, :]
                ms = jnp.mean(x2 * x2, axis=-1, keepdims=True)
                outs.append(x2 * lax.rsqrt(ms + EPS) * gfin)
            for rs, out in zip(rows, outs):
                o_ref[rs, :] = out
            return carry

        lax.fori_loop(0, x_ref.shape[0] // (NORM_ROWS * NORM_UNROLL), body, 0)


def _ffn(x1, g, shift, scale, gate, w1, w2, g_final, seq):
    M, D = x1.shape
    F = w1.shape[1]
    tm, tf = 1024, 1024
    per_b = seq // tm
    bidx = lambda i, j: (i // per_b, 0, 0)
    return pl.pallas_call(
        _ffn_kernel,
        out_shape=jax.ShapeDtypeStruct((M, D), F32),
        grid=(M // tm, F // tf),
        in_specs=[
            pl.BlockSpec((tm, D), lambda i, j: (i, 0)),
            pl.BlockSpec((1, D), lambda i, j: (0, 0)),
            pl.BlockSpec((1, 1, D), bidx),
            pl.BlockSpec((1, 1, D), bidx),
            pl.BlockSpec((1, 1, D), bidx),
            pl.BlockSpec((D, tf), lambda i, j: (0, j)),
            pl.BlockSpec((tf, D), lambda i, j: (j, 0)),
            pl.BlockSpec((1, D), lambda i, j: (0, 0)),
        ],
        out_specs=pl.BlockSpec((tm, D), lambda i, j: (i, 0)),
        scratch_shapes=[pltpu.VMEM((tm, D), BF16)],
        compiler_params=_params(("parallel", "arbitrary"), vmem=FFN_VMEM_LIMIT),
        name="ffn",
    )(x1, g, shift, scale, gate, w1, w2, g_final)


def kernel(x, c, positions, w_ada, b_ada, g_norm_mix, w_in, conv_w, conv_b, conv_ln_g, conv_ln_b,
           w_conv_out, ret_gn_g, ret_gn_b, w_ret_out, w_out, g_norm_ffn, w_ff1, w_ff2, g_norm_final):
    B, S, D = x.shape
    assert w_ada.shape[0] == 1, "kernel supports DEPTH == 1"
    l = 0
    x2d = x.reshape(B * S, D)
    mod = _mod(c, w_ada[l], b_ada[l]).reshape(B, N_MOD, 1, D)
    shift_m, scale_m, gate_m, shift_f, scale_f, gate_f = (mod[:, i] for i in range(N_MOD))
    proj, y_conv, wc, wr, wo, w1, w2 = _inproj(
        x2d, g_norm_mix[l].reshape(1, D), shift_m, scale_m, w_in[l].astype(BF16),
        conv_w[l], conv_b[l], S,
        to_cast=(w_conv_out[l], w_ret_out[l], w_out[l], w_ff1[l], w_ff2[l]))
    a_ret = _ret(proj, positions, ret_gn_g[l], ret_gn_b[l], B, S)
    x1 = _mixout(y_conv, a_ret, proj, x2d, gate_m, conv_ln_g[l], conv_ln_b[l], wc, wr, wo, S)
    out = _ffn(x1, g_norm_ffn[l].reshape(1, D), shift_f, scale_f, gate_f, w1, w2,
               g_norm_final.reshape(1, D), S)
    return out.reshape(B, S, D)
```

```python
import functools

import jax
import jax.numpy as jnp
from jax import lax
from jax.experimental import pallas as pl
from jax.experimental.pallas import tpu as pltpu

D_MODEL = 2048
CONV_DIM = D_MODEL // 2
CONV_WIDTH = 31
RET_HEADS = 8
RET_QK_DIM = D_MODEL // 2
RET_V_DIM = D_MODEL
QK_HEAD = RET_QK_DIM // RET_HEADS
V_HEAD = RET_V_DIM // RET_HEADS
D_FF = 4 * D_MODEL
ROPE_BASE = 10000.0
EPS = 1e-6
IN_COLS = 2 * CONV_DIM + 2 * RET_QK_DIM + 2 * RET_V_DIM + 2 * D_MODEL
N_MOD = 6

RET_CHUNK = 256
RET_ROWS = 128
CONV_HIST = 32
CONV_ROWS = 128
CONV_PIECE_ROWS = 32
CONV_LANES = 128

VMEM_LIMIT = 56 * 1024 * 1024
FFN_VMEM_LIMIT = 60 * 1024 * 1024
BF16 = jnp.bfloat16
F32 = jnp.float32


def _params(sem, vmem=VMEM_LIMIT):
    return pltpu.CompilerParams(dimension_semantics=sem, vmem_limit_bytes=vmem)


def _mod_kernel(c_ref, w_ref, b_ref, o_ref):
    c = c_ref[...]
    ca = (c * jax.nn.sigmoid(c)).astype(BF16)
    o_ref[...] = jnp.dot(ca, w_ref[...].astype(BF16), preferred_element_type=F32) + b_ref[...]


def _mod(c, w_ada, b_ada):
    B, D = c.shape
    N = w_ada.shape[1]
    tn = 1024
    return pl.pallas_call(
        _mod_kernel,
        out_shape=jax.ShapeDtypeStruct((B, N), F32),
        grid=(N // tn,),
        in_specs=[
            pl.BlockSpec((B, D), lambda j: (0, 0)),
            pl.BlockSpec((D, tn), lambda j: (0, j)),
            pl.BlockSpec((1, tn), lambda j: (0, j)),
        ],
        out_specs=pl.BlockSpec((B, tn), lambda j: (0, j)),
        compiler_params=_params(("arbitrary",)),
        name="mod",
    )(c, w_ada, b_ada.reshape(1, N))


NORM_ROWS = 16
NORM_UNROLL = 8


def _row_loop(n_rows, fn):
    def body(r, carry):
        fn(pl.multiple_of(r * NORM_ROWS, NORM_ROWS))
        return carry

    lax.fori_loop(0, n_rows // NORM_ROWS, body, 0, unroll=NORM_UNROLL)


def _rms_modulate_into(x_ref, h_ref, g, shift, scale):
    gs = g * (1.0 + scale)

    def rows(r0):
        x = x_ref[pl.ds(r0, NORM_ROWS), :]
        ms = jnp.mean(x * x, axis=-1, keepdims=True)
        h_ref[pl.ds(r0, NORM_ROWS), :] = (x * lax.rsqrt(ms + EPS) * gs + shift).astype(h_ref.dtype)

    _row_loop(x_ref.shape[0], rows)


def _conv_rows(win_ref, y_ref, w_ref, b_ref, zero_ref):
    SUB = 8
    R, LC = CONV_PIECE_ROWS, CONV_LANES
    zero = zero_ref[...]

    def token(v):
        return lax.bitcast_convert_type(pltpu.roll(v, 1, 1), jnp.int32) & zero

    def after(x, tok):
        if tok is None:
            return x
        return lax.bitcast_convert_type(lax.bitcast_convert_type(x, jnp.int32) | tok[0:x.shape[0]], F32)

    tok = None
    for c in range(CONV_DIM // LC):
        ls = slice(c * LC, (c + 1) * LC)
        tails = [None] * SUB
        for r0 in range(0, CONV_ROWS, R):
            acc = None
            for b in range(SUB):
                lead = SUB if tails[b] is None else 0
                p = None
                for a8 in range(-(-CONV_WIDTH // SUB)):
                    j = SUB * a8 + b
                    if j >= CONV_WIDTH:
                        continue
                    start = r0 + CONV_HIST - lead - SUB * a8
                    k = CONV_WIDTH - 1 - j
                    w = w_ref[k:k + 1, ls]
                    if p is None:
                        w = after(w, tok)
                    term = win_ref[start:start + R + lead, ls] * w
                    p = term if p is None else p + term
                if tails[b] is not None:
                    p = jnp.concatenate([tails[b], p], axis=0)
                tails[b] = p[R:]
                if b:
                    p = pltpu.roll(p, b, 0)
                p = p[SUB:]
                acc = p if acc is None else acc + p
            tok = token(acc[0:8])
            y_ref[r0:r0 + R, ls] = acc + b_ref[:, ls]


GLU_STEP = 2
CONV_FIRST_STEP = 3


def _block_of_step(j, first_step, n_blocks):
    return jnp.clip(j - first_step, 0, n_blocks - 1)


def _inproj_kernel(*refs, tiles_per_seq, cast_steps):
    n_cast = len(cast_steps)
    x_ref, g_ref, sh_ref, sc_ref, w_ref, cw_ref, cb_ref, zero_ref = refs[:8]
    cast_in = refs[8:8 + n_cast]
    o_ref, y_ref = refs[8 + n_cast:10 + n_cast]
    cast_out = refs[10 + n_cast:10 + 2 * n_cast]
    h_ref, stash_ref, vext_ref, win_ref = refs[10 + 2 * n_cast:]
    i = pl.program_id(0)
    j = pl.program_id(1)
    tm = x_ref.shape[0]

    for steps in sorted(set(cast_steps)):
        @pl.when(j < steps)
        def _(steps=steps):
            for src, dst, n in zip(cast_in, cast_out, cast_steps):
                if n == steps:
                    dst[...] = src[...].astype(dst.dtype)

    @pl.when(j == 0)
    def _():
        _rms_modulate_into(x_ref, h_ref, g_ref[...], sh_ref[0], sc_ref[0])

    @pl.when(jnp.logical_and(i == 0, j == 0))
    def _():
        vext_ref[...] = jnp.zeros(vext_ref.shape, F32)
        win_ref[...] = jnp.zeros(win_ref.shape, F32)

    @pl.when(j == GLU_STEP)
    def _():
        first = i % tiles_per_seq == 0

        @pl.when(first)
        def _():
            vext_ref[0:CONV_HIST, :] = jnp.zeros((CONV_HIST, CONV_DIM), F32)

        @pl.when(jnp.logical_not(first))
        def _():
            vext_ref[0:CONV_HIST, :] = vext_ref[tm:tm + CONV_HIST, :]

        a = stash_ref[0].astype(F32)
        gt = stash_ref[1].astype(F32)
        vext_ref[CONV_HIST:CONV_HIST + tm, :] = a * (0.5 * jnp.tanh(0.5 * gt) + 0.5)

    _conv_rows(win_ref, y_ref, cw_ref, cb_ref, zero_ref)

    o_ref[...] = jnp.dot(h_ref[...], w_ref[...], preferred_element_type=F32).astype(o_ref.dtype)

    nxt = _block_of_step(j + 1, CONV_FIRST_STEP, tm // CONV_ROWS)
    win_ref[...] = vext_ref[pl.ds(pl.multiple_of(nxt * CONV_ROWS, CONV_ROWS), CONV_HIST + CONV_ROWS), :]

    @pl.when(j < 2)
    def _():
        stash_ref[j] = o_ref[...]


BF16_SUBLANES = 16


def _inproj(x2d, g, shift, scale, w, conv_w, conv_b, seq, to_cast):
    M, D = x2d.shape
    N = w.shape[1]
    C = CONV_DIM
    tm, tn = 1024, 1024
    per_b = seq // tm
    n_blocks = tm // CONV_ROWS
    n_i, n_j = M // tm, N // tn
    assert tn == C and n_j >= CONV_FIRST_STEP + n_blocks
    const = lambda i, j: (0, 0)
    cast_steps, cast_specs = [], []
    for wk in to_cast:
        steps = 1
        while steps * 2 <= n_j and wk.shape[0] % (n_i * steps * 2 * BF16_SUBLANES) == 0:
            steps *= 2
        rows = wk.shape[0] // (n_i * steps)
        assert rows % BF16_SUBLANES == 0
        cast_steps.append(steps)
        cast_specs.append(((rows, wk.shape[1]),
                           lambda i, j, steps=steps: (i * steps + jnp.minimum(j, steps - 1), 0)))
    return pl.pallas_call(
        functools.partial(_inproj_kernel, tiles_per_seq=per_b, cast_steps=tuple(cast_steps)),
        out_shape=(jax.ShapeDtypeStruct((M, N), BF16), jax.ShapeDtypeStruct((M, C), F32),
                   *(jax.ShapeDtypeStruct(wk.shape, BF16) for wk in to_cast)),
        grid=(n_i, n_j),
        in_specs=[
            pl.BlockSpec((tm, D), lambda i, j: (i, 0)),
            pl.BlockSpec((1, D), const),
            pl.BlockSpec((1, 1, D), lambda i, j: (i // per_b, 0, 0)),
            pl.BlockSpec((1, 1, D), lambda i, j: (i // per_b, 0, 0)),
            pl.BlockSpec((D, tn), lambda i, j: (0, j)),
            pl.BlockSpec((CONV_WIDTH, C), const),
            pl.BlockSpec((1, C), const),
            pl.BlockSpec((1, CONV_LANES), const),
            *(pl.BlockSpec(shape, index) for shape, index in cast_specs),
        ],
        out_specs=(pl.BlockSpec((tm, tn), lambda i, j: (i, j)),
                   pl.BlockSpec((CONV_ROWS, C),
                                lambda i, j: (i * n_blocks + _block_of_step(j, CONV_FIRST_STEP, n_blocks), 0)),
                   *(pl.BlockSpec(shape, index) for shape, index in cast_specs)),
        scratch_shapes=[
            pltpu.VMEM((tm, D), BF16),
            pltpu.VMEM((2, tm, C), BF16),
            pltpu.VMEM((CONV_HIST + tm, C), F32),
            pltpu.VMEM((CONV_HIST + CONV_ROWS, C), F32),
        ],
        compiler_params=_params(("arbitrary", "arbitrary")),
        name="in_proj",
    )(x2d, g, shift, scale, w, conv_w, conv_b.reshape(1, C), jnp.zeros((1, CONV_LANES), jnp.int32), *to_cast)


def _ret_kernel(pos_ref, q_ref, k_ref, v_ref, g_ref, invf_ref, sgn_ref, dq_ref, dk_ref,
                cd_ref, gng_ref, gnb_ref, o_ref, r_ref, *, ts):
    s = pl.program_id(1)

    @pl.when(s == 0)
    def _():
        r_ref[...] = jnp.zeros(r_ref.shape, F32)

    half = QK_HEAD // 2
    pos = pos_ref[...].astype(F32)
    invf = invf_ref[...]
    low = lax.broadcasted_iota(jnp.int32, (ts // 2, QK_HEAD), 1) < half
    ang = jnp.where(low, pos[:ts // 2] * invf, pos[ts // 2:] * invf)
    c2, s2 = jnp.cos(ang), jnp.sin(ang)
    c2r, s2r = pltpu.roll(c2, half, 1), pltpu.roll(s2, half, 1)
    cosf = jnp.concatenate([jnp.where(low, c2, c2r), jnp.where(low, c2r, c2)], axis=0)
    sinf = jnp.concatenate([jnp.where(low, s2, s2r), jnp.where(low, s2r, s2)], axis=0) * sgn_ref[...]
    C, SB = RET_CHUNK, RET_ROWS
    causal = []
    for m in range(C // SB):
        row = lax.broadcasted_iota(jnp.int32, (SB, (m + 1) * SB), 0) + m * SB
        col = lax.broadcasted_iota(jnp.int32, (SB, (m + 1) * SB), 1)
        causal.append(col <= row)
    def scores(n):
        rs = slice(n * C, (n + 1) * C)
        cs_, sn_ = cosf[rs], sinf[rs]
        qds, kds, ps = [], [], []
        for h in range(RET_HEADS):
            qs = slice(h * QK_HEAD, (h + 1) * QK_HEAD)
            q = q_ref[rs, qs].astype(F32)
            k = k_ref[rs, qs].astype(F32)
            qd = ((q * cs_ + pltpu.roll(q, QK_HEAD // 2, 1) * sn_) * dq_ref[h]).astype(BF16)
            kd = ((k * cs_ + pltpu.roll(k, QK_HEAD // 2, 1) * sn_) * dk_ref[h]).astype(BF16)
            qds.append(qd)
            kds.append(kd)
            for m in range(C // SB):
                kcols = (m + 1) * SB
                sc = lax.dot_general(qd[m * SB:(m + 1) * SB], kd[:kcols], (((1,), (1,)), ((), ())),
                                     preferred_element_type=F32)
                ps.append(jnp.where(causal[m], sc, 0.0).astype(BF16))
        return qds, kds, ps

    def outputs_and_state(n, qds, kds, ps):
        rs = slice(n * C, (n + 1) * C)
        for h in range(RET_HEADS):
            vs = slice(h * V_HEAD, (h + 1) * V_HEAD)
            v = v_ref[rs, vs]
            Rb = r_ref[h].astype(BF16)
            gng = gng_ref[:, vs]
            gnb = gnb_ref[:, vs]
            for m in range(C // SB):
                kcols = (m + 1) * SB
                lhs = jnp.concatenate([ps[h * (C // SB) + m], qds[h][m * SB:(m + 1) * SB]], axis=1)
                rhs = jnp.concatenate([v[:kcols], Rb], axis=0)
                o = jnp.dot(lhs, rhs, preferred_element_type=F32)
                mu = jnp.mean(o, axis=-1, keepdims=True)
                oc = o - mu
                var = jnp.mean(oc * oc, axis=-1, keepdims=True)
                on = oc * lax.rsqrt(var + EPS) * gng + gnb
                orow = slice(n * C + m * SB, n * C + (m + 1) * SB)
                g = g_ref[orow, vs]
                o_ref[orow, vs] = g * jax.nn.sigmoid(g) * on.astype(o_ref.dtype)
        for h in range(RET_HEADS):
            vs = slice(h * V_HEAD, (h + 1) * V_HEAD)
            kv = lax.dot_general(kds[h], v_ref[rs, vs], (((0,), (0,)), ((), ())),
                                 preferred_element_type=F32)
            r_ref[h] = (r_ref[h] + kv) * cd_ref[h]

    n_chunks = ts // C
    ahead = scores(0)
    for n in range(n_chunks):
        cur = ahead
        if n + 1 < n_chunks:
            ahead = scores(n + 1)
        outputs_and_state(n, *cur)


def _ret_tables():
    C = RET_CHUNK
    log_gamma = jnp.log(1.0 - jnp.exp2(-5.0 - jnp.arange(RET_HEADS, dtype=F32)))
    idx = jnp.arange(C, dtype=F32)
    dq = jnp.exp((idx + 1.0)[None, :] * log_gamma[:, None])
    dk = jnp.exp(-(idx + 1.0)[None, :] * log_gamma[:, None]) * (QK_HEAD ** -0.5)
    cd = jnp.exp(C * log_gamma)
    half = QK_HEAD // 2
    inv_freq = ROPE_BASE ** (-jnp.arange(0, half, dtype=F32) / half)
    invf = jnp.concatenate([inv_freq, inv_freq]).reshape(1, QK_HEAD)
    sgn = jnp.concatenate([-jnp.ones((half,), F32), jnp.ones((half,), F32)]).reshape(1, QK_HEAD)
    cd_b = jnp.broadcast_to(cd[:, None, None], (RET_HEADS, 1, V_HEAD))
    lanes = (RET_HEADS, C, QK_HEAD)
    return (invf, sgn, jnp.broadcast_to(dq[:, :, None], lanes), jnp.broadcast_to(dk[:, :, None], lanes), cd_b)


def _ret(proj, positions, gn_g, gn_b, batch, seq):
    M = proj.shape[0]
    ts = 512
    per_b = seq // ts
    C = RET_CHUNK
    invf, sgn, dq, dk, cd = _ret_tables()
    row = lambda b, s: b * per_b + s
    const2 = lambda b, s: (0, 0)
    const3 = lambda b, s: (0, 0, 0)
    return pl.pallas_call(
        functools.partial(_ret_kernel, ts=ts),
        out_shape=jax.ShapeDtypeStruct((M, RET_V_DIM), BF16),
        grid=(batch, per_b),
        in_specs=[
            pl.BlockSpec((ts, 1), lambda b, s: (row(b, s), 0)),
            pl.BlockSpec((ts, RET_QK_DIM), lambda b, s: (row(b, s), 2)),
            pl.BlockSpec((ts, RET_QK_DIM), lambda b, s: (row(b, s), 3)),
            pl.BlockSpec((ts, RET_V_DIM), lambda b, s: (row(b, s), 2)),
            pl.BlockSpec((ts, RET_V_DIM), lambda b, s: (row(b, s), 3)),
            pl.BlockSpec((1, QK_HEAD), const2),
            pl.BlockSpec((1, QK_HEAD), const2),
            pl.BlockSpec((RET_HEADS, C, QK_HEAD), const3),
            pl.BlockSpec((RET_HEADS, C, QK_HEAD), const3),
            pl.BlockSpec((RET_HEADS, 1, V_HEAD), const3),
            pl.BlockSpec((1, RET_V_DIM), const2),
            pl.BlockSpec((1, RET_V_DIM), const2),
        ],
        out_specs=pl.BlockSpec((ts, RET_V_DIM), lambda b, s: (row(b, s), 0)),
        scratch_shapes=[pltpu.VMEM((RET_HEADS, QK_HEAD, V_HEAD), F32)],
        compiler_params=_params(("parallel", "arbitrary")),
        name="retention",
    )(positions.reshape(M, 1), proj, proj, proj, proj, invf, sgn, dq, dk, cd,
      gn_g.reshape(1, RET_V_DIM), gn_b.reshape(1, RET_V_DIM))


def _mixout_kernel(yconv_ref, ar_ref, gc_ref, gr_ref, x_ref, gm_ref, lg_ref, lb_ref, wc_ref, wr_ref, wo_ref,
                   o_ref):
    yr = jnp.dot(ar_ref[...], wr_ref[...], preferred_element_type=F32)
    lg = lg_ref[...]
    lb = lb_ref[...]
    groups = []
    for q in range(0, yconv_ref.shape[0], NORM_ROWS):
        y = yconv_ref[q:q + NORM_ROWS, :]
        mu = jnp.mean(y, axis=-1, keepdims=True)
        yc = y - mu
        var = jnp.mean(yc * yc, axis=-1, keepdims=True)
        yn = yc * lax.rsqrt(var + EPS) * lg + lb
        groups.append((yn * jax.nn.sigmoid(yn)).astype(BF16))
    a_conv = jnp.concatenate(groups, axis=0)
    yc = jnp.dot(a_conv, wc_ref[...], preferred_element_type=F32)
    merged = (jax.nn.sigmoid(gc_ref[...].astype(F32)) * yc
              + jax.nn.sigmoid(gr_ref[...].astype(F32)) * yr)
    out = jnp.dot(merged.astype(BF16), wo_ref[...], preferred_element_type=F32)
    o_ref[...] = x_ref[...] + gm_ref[0] * out


def _mixout(y_conv, a_ret, proj, x2d, gate_m, ln_g, ln_b, wc, wr, wo, seq):
    M, D = x2d.shape
    tm = 256
    per_b = seq // tm
    resident = functools.partial(pl.BlockSpec, pipeline_mode=pl.Buffered(1))
    const = lambda i: (0, 0)
    return pl.pallas_call(
        _mixout_kernel,
        out_shape=jax.ShapeDtypeStruct((M, D), F32),
        grid=(M // tm,),
        in_specs=[
            pl.BlockSpec((tm, CONV_DIM), lambda i: (i, 0)),
            pl.BlockSpec((tm, RET_V_DIM), lambda i: (i, 0)),
            pl.BlockSpec((tm, D), lambda i: (i, 4)),
            pl.BlockSpec((tm, D), lambda i: (i, 5)),
            pl.BlockSpec((tm, D), lambda i: (i, 0)),
            pl.BlockSpec((1, 1, D), lambda i: (i // per_b, 0, 0)),
            pl.BlockSpec((1, CONV_DIM), const),
            pl.BlockSpec((1, CONV_DIM), const),
            resident((CONV_DIM, D), lambda i: (0, 0)),
            resident((RET_V_DIM, D), lambda i: (0, 0)),
            resident((D, D), lambda i: (0, 0)),
        ],
        out_specs=pl.BlockSpec((tm, D), lambda i: (i, 0)),
        compiler_params=_params(("parallel",)),
        name="mix_out",
    )(y_conv, a_ret, proj, proj, x2d, gate_m, ln_g.reshape(1, CONV_DIM), ln_b.reshape(1, CONV_DIM), wc, wr, wo)


def _ffn_kernel(x_ref, g_ref, sh_ref, sc_ref, gf_ref, w1_ref, w2_ref, gfin_ref, o_ref, h_ref):
    j = pl.program_id(1)

    def chunk(rows=slice(None)):
        hid = jnp.dot(h_ref[rows, :], w1_ref[...], preferred_element_type=F32)
        hid = jnp.square(jnp.maximum(hid, 0.0)).astype(BF16)
        return jnp.dot(hid, w2_ref[...], preferred_element_type=F32)

    @pl.when(j == 0)
    def _():
        gs = g_ref[...] * (1.0 + sc_ref[0])
        shift = sh_ref[0]
        half = x_ref.shape[0] // 2
        for r_lo in (0, half):
            for r0 in range(r_lo, r_lo + half, NORM_ROWS):
                x = x_ref[r0:r0 + NORM_ROWS, :]
                ms = jnp.mean(x * x, axis=-1, keepdims=True)
                h_ref[r0:r0 + NORM_ROWS, :] = (x * lax.rsqrt(ms + EPS) * gs + shift).astype(h_ref.dtype)
            o_ref[r_lo:r_lo + half, :] = chunk(slice(r_lo, r_lo + half))

    last = pl.num_programs(1) - 1

    @pl.when(jnp.logical_and(j != 0, j != last))
    def _():
        o_ref[...] += chunk()

    @pl.when(j == last)
    def _():
        gf = gf_ref[0]
        gfin = gfin_ref[...]
        half = x_ref.shape[0] // 2
        for r_lo in (0, half):
            o_ref[r_lo:r_lo + half, :] += chunk(slice(r_lo, r_lo + half))
            for r0 in range(r_lo, r_lo + half, NORM_ROWS):
                rs = slice(r0, r0 + NORM_ROWS)
                x2 = x_ref[rs, :] + gf * o_ref[rs, :]
                ms = jnp.mean(x2 * x2, axis=-1, keepdims=True)
                o_ref[rs, :] = x2 * lax.rsqrt(ms + EPS) * gfin


def _ffn(x1, g, shift, scale, gate, w1, w2, g_final, seq):
    M, D = x1.shape
    F = w1.shape[1]
    tm, tf = 1024, 1024
    per_b = seq // tm
    bidx = lambda i, j: (i // per_b, 0, 0)
    return pl.pallas_call(
        _ffn_kernel,
        out_shape=jax.ShapeDtypeStruct((M, D), F32),
        grid=(M // tm, F // tf),
        in_specs=[
            pl.BlockSpec((tm, D), lambda i, j: (i, 0)),
            pl.BlockSpec((1, D), lambda i, j: (0, 0)),
            pl.BlockSpec((1, 1, D), bidx),
            pl.BlockSpec((1, 1, D), bidx),
            pl.BlockSpec((1, 1, D), bidx),
            pl.BlockSpec((D, tf), lambda i, j: (0, j)),
            pl.BlockSpec((tf, D), lambda i, j: (j, 0)),
            pl.BlockSpec((1, D), lambda i, j: (0, 0)),
        ],
        out_specs=pl.BlockSpec((tm, D), lambda i, j: (i, 0)),
        scratch_shapes=[pltpu.VMEM((tm, D), BF16)],
        compiler_params=_params(("parallel", "arbitrary"), vmem=FFN_VMEM_LIMIT),
        name="ffn",
    )(x1, g, shift, scale, gate, w1, w2, g_final)


def kernel(x, c, positions, w_ada, b_ada, g_norm_mix, w_in, conv_w, conv_b, conv_ln_g, conv_ln_b,
           w_conv_out, ret_gn_g, ret_gn_b, w_ret_out, w_out, g_norm_ffn, w_ff1, w_ff2, g_norm_final):
    B, S, D = x.shape
    assert w_ada.shape[0] == 1, "kernel supports DEPTH == 1"
    l = 0
    x2d = x.reshape(B * S, D)
    mod = _mod(c, w_ada[l], b_ada[l]).reshape(B, N_MOD, 1, D)
    shift_m, scale_m, gate_m, shift_f, scale_f, gate_f = (mod[:, i] for i in range(N_MOD))
    proj, y_conv, wc, wr, wo, w1, w2 = _inproj(
        x2d, g_norm_mix[l].reshape(1, D), shift_m, scale_m, w_in[l].astype(BF16),
        conv_w[l], conv_b[l], S,
        to_cast=(w_conv_out[l], w_ret_out[l], w_out[l], w_ff1[l], w_ff2[l]))
    a_ret = _ret(proj, positions, ret_gn_g[l], ret_gn_b[l], B, S)
    x1 = _mixout(y_conv, a_ret, proj, x2d, gate_m, conv_ln_g[l], conv_ln_b[l], wc, wr, wo, S)
    out = _ffn(x1, g_norm_ffn[l].reshape(1, D), shift_f, scale_f, gate_f, w1, w2,
               g_norm_final.reshape(1, D), S)
    return out.reshape(B, S, D)
```

```python
import functools

import jax
import jax.numpy as jnp
from jax import lax
from jax.experimental import pallas as pl
from jax.experimental.pallas import tpu as pltpu

D_MODEL = 2048
CONV_DIM = D_MODEL // 2
CONV_WIDTH = 31
RET_HEADS = 8
RET_QK_DIM = D_MODEL // 2
RET_V_DIM = D_MODEL
QK_HEAD = RET_QK_DIM // RET_HEADS
V_HEAD = RET_V_DIM // RET_HEADS
D_FF = 4 * D_MODEL
ROPE_BASE = 10000.0
EPS = 1e-6
IN_COLS = 2 * CONV_DIM + 2 * RET_QK_DIM + 2 * RET_V_DIM + 2 * D_MODEL
N_MOD = 6

RET_CHUNK = 256
RET_ROWS = 128
CONV_HIST = 32
CONV_ROWS = 128
CONV_PIECE_ROWS = 32
CONV_LANES = 128

VMEM_LIMIT = 56 * 1024 * 1024
FFN_VMEM_LIMIT = 60 * 1024 * 1024
BF16 = jnp.bfloat16
F32 = jnp.float32


def _params(sem, vmem=VMEM_LIMIT):
    return pltpu.CompilerParams(dimension_semantics=sem, vmem_limit_bytes=vmem)


def _mod_kernel(c_ref, w_ref, b_ref, o_ref):
    c = c_ref[...]
    ca = (c * jax.nn.sigmoid(c)).astype(BF16)
    o_ref[...] = jnp.dot(ca, w_ref[...].astype(BF16), preferred_element_type=F32) + b_ref[...]


def _mod(c, w_ada, b_ada):
    B, D = c.shape
    N = w_ada.shape[1]
    tn = 1024
    return pl.pallas_call(
        _mod_kernel,
        out_shape=jax.ShapeDtypeStruct((B, N), F32),
        grid=(N // tn,),
        in_specs=[
            pl.BlockSpec((B, D), lambda j: (0, 0)),
            pl.BlockSpec((D, tn), lambda j: (0, j)),
            pl.BlockSpec((1, tn), lambda j: (0, j)),
        ],
        out_specs=pl.BlockSpec((B, tn), lambda j: (0, j)),
        compiler_params=_params(("arbitrary",)),
        name="mod",
    )(c, w_ada, b_ada.reshape(1, N))


NORM_ROWS = 16


def _rms_modulate_rows(x_ref, h_ref, gs, shift, r_lo, r_hi):
    for r0 in range(r_lo, r_hi, NORM_ROWS):
        x = x_ref[r0:r0 + NORM_ROWS, :]
        ms = jnp.mean(x * x, axis=-1, keepdims=True)
        h_ref[r0:r0 + NORM_ROWS, :] = (x * lax.rsqrt(ms + EPS) * gs + shift).astype(h_ref.dtype)


def _conv_rows(win_ref, y_ref, w_ref, b_ref, zero_ref):
    SUB = 8
    R, LC = CONV_PIECE_ROWS, CONV_LANES
    zero = zero_ref[...]

    def token(v):
        return lax.bitcast_convert_type(pltpu.roll(v, 1, 1), jnp.int32) & zero

    def after(x, tok):
        if tok is None:
            return x
        return lax.bitcast_convert_type(lax.bitcast_convert_type(x, jnp.int32) | tok[0:x.shape[0]], F32)

    tok = None
    for c in range(CONV_DIM // LC):
        ls = slice(c * LC, (c + 1) * LC)
        tails = [None] * SUB
        for r0 in range(0, CONV_ROWS, R):
            acc = None
            for b in range(SUB):
                lead = SUB if tails[b] is None else 0
                p = None
                for a8 in range(-(-CONV_WIDTH // SUB)):
                    j = SUB * a8 + b
                    if j >= CONV_WIDTH:
                        continue
                    start = r0 + CONV_HIST - lead - SUB * a8
                    k = CONV_WIDTH - 1 - j
                    w = w_ref[k:k + 1, ls]
                    if p is None:
                        w = after(w, tok)
                    term = win_ref[start:start + R + lead, ls] * w
                    p = term if p is None else p + term
                if tails[b] is not None:
                    p = jnp.concatenate([tails[b], p], axis=0)
                tails[b] = p[R:]
                if b:
                    p = pltpu.roll(p, b, 0)
                p = p[SUB:]
                acc = p if acc is None else acc + p
            tok = token(acc[0:8])
            y_ref[r0:r0 + R, ls] = acc + b_ref[:, ls]


GLU_STEP = 2
CONV_FIRST_STEP = 3


def _block_of_step(j, first_step, n_blocks):
    return jnp.clip(j - first_step, 0, n_blocks - 1)


def _inproj_kernel(*refs, tiles_per_seq, cast_steps):
    n_cast = len(cast_steps)
    x_ref, g_ref, sh_ref, sc_ref, w_ref, cw_ref, cb_ref, zero_ref = refs[:8]
    cast_in = refs[8:8 + n_cast]
    o_ref, y_ref = refs[8 + n_cast:10 + n_cast]
    cast_out = refs[10 + n_cast:10 + 2 * n_cast]
    h_ref, stash_ref, vext_ref, win_ref = refs[10 + 2 * n_cast:]
    i = pl.program_id(0)
    j = pl.program_id(1)
    tm = x_ref.shape[0]

    for steps in sorted(set(cast_steps)):
        @pl.when(j < steps)
        def _(steps=steps):
            for src, dst, n in zip(cast_in, cast_out, cast_steps):
                if n == steps:
                    dst[...] = src[...].astype(dst.dtype)

    @pl.when(jnp.logical_and(i == 0, j == 0))
    def _():
        vext_ref[...] = jnp.zeros(vext_ref.shape, F32)
        win_ref[...] = jnp.zeros(win_ref.shape, F32)

    @pl.when(j == GLU_STEP)
    def _():
        first = i % tiles_per_seq == 0

        @pl.when(first)
        def _():
            vext_ref[0:CONV_HIST, :] = jnp.zeros((CONV_HIST, CONV_DIM), F32)

        @pl.when(jnp.logical_not(first))
        def _():
            vext_ref[0:CONV_HIST, :] = vext_ref[tm:tm + CONV_HIST, :]

        a = stash_ref[0].astype(F32)
        gt = stash_ref[1].astype(F32)
        vext_ref[CONV_HIST:CONV_HIST + tm, :] = a * (0.5 * jnp.tanh(0.5 * gt) + 0.5)

    def conv_and_dot(with_prologue):
        _conv_rows(win_ref, y_ref, cw_ref, cb_ref, zero_ref)
        if not with_prologue:
            o_ref[...] = jnp.dot(h_ref[...], w_ref[...], preferred_element_type=F32).astype(o_ref.dtype)
            return
        gs = g_ref[...] * (1.0 + sc_ref[0])
        shift = sh_ref[0]
        half = tm // 2
        for r_lo in (0, half):
            _rms_modulate_rows(x_ref, h_ref, gs, shift, r_lo, r_lo + half)
            o_ref[r_lo:r_lo + half, :] = jnp.dot(
                h_ref[r_lo:r_lo + half, :], w_ref[...], preferred_element_type=F32).astype(o_ref.dtype)

    @pl.when(j == 0)
    def _():
        conv_and_dot(True)

    @pl.when(j != 0)
    def _():
        conv_and_dot(False)

    nxt = _block_of_step(j + 1, CONV_FIRST_STEP, tm // CONV_ROWS)
    win_ref[...] = vext_ref[pl.ds(pl.multiple_of(nxt * CONV_ROWS, CONV_ROWS), CONV_HIST + CONV_ROWS), :]

    @pl.when(j < 2)
    def _():
        stash_ref[j] = o_ref[...]


BF16_SUBLANES = 16


def _inproj(x2d, g, shift, scale, w, conv_w, conv_b, seq, to_cast):
    M, D = x2d.shape
    N = w.shape[1]
    C = CONV_DIM
    tm, tn = 1024, 1024
    per_b = seq // tm
    n_blocks = tm // CONV_ROWS
    n_i, n_j = M // tm, N // tn
    assert tn == C and n_j >= CONV_FIRST_STEP + n_blocks
    const = lambda i, j: (0, 0)
    cast_steps, cast_specs = [], []
    for wk in to_cast:
        steps = 1
        while steps * 2 <= n_j and wk.shape[0] % (n_i * steps * 2 * BF16_SUBLANES) == 0:
            steps *= 2
        rows = wk.shape[0] // (n_i * steps)
        assert rows % BF16_SUBLANES == 0
        cast_steps.append(steps)
        cast_specs.append(((rows, wk.shape[1]),
                           lambda i, j, steps=steps: (i * steps + jnp.minimum(j, steps - 1), 0)))
    return pl.pallas_call(
        functools.partial(_inproj_kernel, tiles_per_seq=per_b, cast_steps=tuple(cast_steps)),
        out_shape=(jax.ShapeDtypeStruct((M, N), BF16), jax.ShapeDtypeStruct((M, C), F32),
                   *(jax.ShapeDtypeStruct(wk.shape, BF16) for wk in to_cast)),
        grid=(n_i, n_j),
        in_specs=[
            pl.BlockSpec((tm, D), lambda i, j: (i, 0)),
            pl.BlockSpec((1, D), const),
            pl.BlockSpec((1, 1, D), lambda i, j: (i // per_b, 0, 0)),
            pl.BlockSpec((1, 1, D), lambda i, j: (i // per_b, 0, 0)),
            pl.BlockSpec((D, tn), lambda i, j: (0, j)),
            pl.BlockSpec((CONV_WIDTH, C), const),
            pl.BlockSpec((1, C), const),
            pl.BlockSpec((1, CONV_LANES), const),
            *(pl.BlockSpec(shape, index) for shape, index in cast_specs),
        ],
        out_specs=(pl.BlockSpec((tm, tn), lambda i, j: (i, j)),
                   pl.BlockSpec((CONV_ROWS, C),
                                lambda i, j: (i * n_blocks + _block_of_step(j, CONV_FIRST_STEP, n_blocks), 0)),
                   *(pl.BlockSpec(shape, index) for shape, index in cast_specs)),
        scratch_shapes=[
            pltpu.VMEM((tm, D), BF16),
            pltpu.VMEM((2, tm, C), BF16),
            pltpu.VMEM((CONV_HIST + tm, C), F32),
            pltpu.VMEM((CONV_HIST + CONV_ROWS, C), F32),
        ],
        compiler_params=_params(("arbitrary", "arbitrary")),
        name="in_proj",
    )(x2d, g, shift, scale, w, conv_w, conv_b.reshape(1, C), jnp.zeros((1, CONV_LANES), jnp.int32), *to_cast)


def _ret_kernel(pos_ref, q_ref, k_ref, v_ref, g_ref, invf_ref, sgn_ref, dq_ref, dk_ref,
                cd_ref, gng_ref, gnb_ref, o_ref, r_ref, *, ts):
    s = pl.program_id(1)

    @pl.when(s == 0)
    def _():
        r_ref[...] = jnp.zeros(r_ref.shape, F32)

    half = QK_HEAD // 2
    pos = pos_ref[...].astype(F32)
    invf = invf_ref[...]
    low = lax.broadcasted_iota(jnp.int32, (ts // 2, QK_HEAD), 1) < half
    ang = jnp.where(low, pos[:ts // 2] * invf, pos[ts // 2:] * invf)
    c2, s2 = jnp.cos(ang), jnp.sin(ang)
    c2r, s2r = pltpu.roll(c2, half, 1), pltpu.roll(s2, half, 1)
    cosf = jnp.concatenate([jnp.where(low, c2, c2r), jnp.where(low, c2r, c2)], axis=0)
    sinf = jnp.concatenate([jnp.where(low, s2, s2r), jnp.where(low, s2r, s2)], axis=0) * sgn_ref[...]
    C, SB = RET_CHUNK, RET_ROWS
    causal = []
    for m in range(C // SB):
        row = lax.broadcasted_iota(jnp.int32, (SB, (m + 1) * SB), 0) + m * SB
        col = lax.broadcasted_iota(jnp.int32, (SB, (m + 1) * SB), 1)
        causal.append(col <= row)
    def scores(n):
        rs = slice(n * C, (n + 1) * C)
        cs_, sn_ = cosf[rs], sinf[rs]
        qds, kds, ps = [], [], []
        for h in range(RET_HEADS):
            qs = slice(h * QK_HEAD, (h + 1) * QK_HEAD)
            q = q_ref[rs, qs].astype(F32)
            k = k_ref[rs, qs].astype(F32)
            qd = ((q * cs_ + pltpu.roll(q, QK_HEAD // 2, 1) * sn_) * dq_ref[h]).astype(BF16)
            kd = ((k * cs_ + pltpu.roll(k, QK_HEAD // 2, 1) * sn_) * dk_ref[h]).astype(BF16)
            qds.append(qd)
            kds.append(kd)
            for m in range(C // SB):
                kcols = (m + 1) * SB
                sc = lax.dot_general(qd[m * SB:(m + 1) * SB], kd[:kcols], (((1,), (1,)), ((), ())),
                                     preferred_element_type=F32)
                ps.append(jnp.where(causal[m], sc, 0.0).astype(BF16))
        return qds, kds, ps

    def outputs_and_state(n, qds, kds, ps):
        rs = slice(n * C, (n + 1) * C)
        for h in range(RET_HEADS):
            vs = slice(h * V_HEAD, (h + 1) * V_HEAD)
            v = v_ref[rs, vs]
            Rb = r_ref[h].astype(BF16)
            gng = gng_ref[:, vs]
            gnb = gnb_ref[:, vs]
            for m in range(C // SB):
                kcols = (m + 1) * SB
                lhs = jnp.concatenate([ps[h * (C // SB) + m], qds[h][m * SB:(m + 1) * SB]], axis=1)
                rhs = jnp.concatenate([v[:kcols], Rb], axis=0)
                o = jnp.dot(lhs, rhs, preferred_element_type=F32)
                mu = jnp.mean(o, axis=-1, keepdims=True)
                oc = o - mu
                var = jnp.mean(oc * oc, axis=-1, keepdims=True)
                on = oc * lax.rsqrt(var + EPS) * gng + gnb
                orow = slice(n * C + m * SB, n * C + (m + 1) * SB)
                g = g_ref[orow, vs]
                o_ref[orow, vs] = g * jax.nn.sigmoid(g) * on.astype(o_ref.dtype)
        for h in range(RET_HEADS):
            vs = slice(h * V_HEAD, (h + 1) * V_HEAD)
            kv = lax.dot_general(kds[h], v_ref[rs, vs], (((0,), (0,)), ((), ())),
                                 preferred_element_type=F32)
            r_ref[h] = (r_ref[h] + kv) * cd_ref[h]

    n_chunks = ts // C
    ahead = scores(0)
    for n in range(n_chunks):
        cur = ahead
        if n + 1 < n_chunks:
            ahead = scores(n + 1)
        outputs_and_state(n, *cur)


def _ret_tables():
    C = RET_CHUNK
    log_gamma = jnp.log(1.0 - jnp.exp2(-5.0 - jnp.arange(RET_HEADS, dtype=F32)))
    idx = jnp.arange(C, dtype=F32)
    dq = jnp.exp((idx + 1.0)[None, :] * log_gamma[:, None])
    dk = jnp.exp(-(idx + 1.0)[None, :] * log_gamma[:, None]) * (QK_HEAD ** -0.5)
    cd = jnp.exp(C * log_gamma)
    half = QK_HEAD // 2
    inv_freq = ROPE_BASE ** (-jnp.arange(0, half, dtype=F32) / half)
    invf = jnp.concatenate([inv_freq, inv_freq]).reshape(1, QK_HEAD)
    sgn = jnp.concatenate([-jnp.ones((half,), F32), jnp.ones((half,), F32)]).reshape(1, QK_HEAD)
    cd_b = jnp.broadcast_to(cd[:, None, None], (RET_HEADS, 1, V_HEAD))
    lanes = (RET_HEADS, C, QK_HEAD)
    return (invf, sgn, jnp.broadcast_to(dq[:, :, None], lanes), jnp.broadcast_to(dk[:, :, None], lanes), cd_b)


def _ret(proj, positions, gn_g, gn_b, batch, seq):
    M = proj.shape[0]
    ts = 512
    per_b = seq // ts
    C = RET_CHUNK
    invf, sgn, dq, dk, cd = _ret_tables()
    row = lambda b, s: b * per_b + s
    const2 = lambda b, s: (0, 0)
    const3 = lambda b, s: (0, 0, 0)
    return pl.pallas_call(
        functools.partial(_ret_kernel, ts=ts),
        out_shape=jax.ShapeDtypeStruct((M, RET_V_DIM), BF16),
        grid=(batch, per_b),
        in_specs=[
            pl.BlockSpec((ts, 1), lambda b, s: (row(b, s), 0)),
            pl.BlockSpec((ts, RET_QK_DIM), lambda b, s: (row(b, s), 2)),
            pl.BlockSpec((ts, RET_QK_DIM), lambda b, s: (row(b, s), 3)),
            pl.BlockSpec((ts, RET_V_DIM), lambda b, s: (row(b, s), 2)),
            pl.BlockSpec((ts, RET_V_DIM), lambda b, s: (row(b, s), 3)),
            pl.BlockSpec((1, QK_HEAD), const2),
            pl.BlockSpec((1, QK_HEAD), const2),
            pl.BlockSpec((RET_HEADS, C, QK_HEAD), const3),
            pl.BlockSpec((RET_HEADS, C, QK_HEAD), const3),
            pl.BlockSpec((RET_HEADS, 1, V_HEAD), const3),
            pl.BlockSpec((1, RET_V_DIM), const2),
            pl.BlockSpec((1, RET_V_DIM), const2),
        ],
        out_specs=pl.BlockSpec((ts, RET_V_DIM), lambda b, s: (row(b, s), 0)),
        scratch_shapes=[pltpu.VMEM((RET_HEADS, QK_HEAD, V_HEAD), F32)],
        compiler_params=_params(("parallel", "arbitrary")),
        name="retention",
    )(positions.reshape(M, 1), proj, proj, proj, proj, invf, sgn, dq, dk, cd,
      gn_g.reshape(1, RET_V_DIM), gn_b.reshape(1, RET_V_DIM))


def _mixout_kernel(yconv_ref, ar_ref, gc_ref, gr_ref, x_ref, gm_ref, lg_ref, lb_ref, wc_ref, wr_ref, wo_ref,
                   o_ref):
    yr = jnp.dot(ar_ref[...], wr_ref[...], preferred_element_type=F32)
    lg = lg_ref[...]
    lb = lb_ref[...]
    groups = []
    for q in range(0, yconv_ref.shape[0], NORM_ROWS):
        y = yconv_ref[q:q + NORM_ROWS, :]
        mu = jnp.mean(y, axis=-1, keepdims=True)
        yc = y - mu
        var = jnp.mean(yc * yc, axis=-1, keepdims=True)
        yn = yc * lax.rsqrt(var + EPS) * lg + lb
        groups.append((yn * jax.nn.sigmoid(yn)).astype(BF16))
    a_conv = jnp.concatenate(groups, axis=0)
    yc = jnp.dot(a_conv, wc_ref[...], preferred_element_type=F32)
    merged = (jax.nn.sigmoid(gc_ref[...].astype(F32)) * yc
              + jax.nn.sigmoid(gr_ref[...].astype(F32)) * yr)
    out = jnp.dot(merged.astype(BF16), wo_ref[...], preferred_element_type=F32)
    o_ref[...] = x_ref[...] + gm_ref[0] * out


def _mixout(y_conv, a_ret, proj, x2d, gate_m, ln_g, ln_b, wc, wr, wo, seq):
    M, D = x2d.shape
    tm = 256
    per_b = seq // tm
    resident = functools.partial(pl.BlockSpec, pipeline_mode=pl.Buffered(1))
    const = lambda i: (0, 0)
    return pl.pallas_call(
        _mixout_kernel,
        out_shape=jax.ShapeDtypeStruct((M, D), F32),
        grid=(M // tm,),
        in_specs=[
            pl.BlockSpec((tm, CONV_DIM), lambda i: (i, 0)),
            pl.BlockSpec((tm, RET_V_DIM), lambda i: (i, 0)),
            pl.BlockSpec((tm, D), lambda i: (i, 4)),
            pl.BlockSpec((tm, D), lambda i: (i, 5)),
            pl.BlockSpec((tm, D), lambda i: (i, 0)),
            pl.BlockSpec((1, 1, D), lambda i: (i // per_b, 0, 0)),
            pl.BlockSpec((1, CONV_DIM), const),
            pl.BlockSpec((1, CONV_DIM), const),
            resident((CONV_DIM, D), lambda i: (0, 0)),
            resident((RET_V_DIM, D), lambda i: (0, 0)),
            resident((D, D), lambda i: (0, 0)),
        ],
        out_specs=pl.BlockSpec((tm, D), lambda i: (i, 0)),
        compiler_params=_params(("parallel",)),
        name="mix_out",
    )(y_conv, a_ret, proj, proj, x2d, gate_m, ln_g.reshape(1, CONV_DIM), ln_b.reshape(1, CONV_DIM), wc, wr, wo)


def _ffn_kernel(x_ref, g_ref, sh_ref, sc_ref, gf_ref, w1_ref, w2_ref, gfin_ref, o_ref, h_ref):
    j = pl.program_id(1)

    def chunk(rows=slice(None)):
        hid = jnp.dot(h_ref[rows, :], w1_ref[...], preferred_element_type=F32)
        hid = jnp.square(jnp.maximum(hid, 0.0)).astype(BF16)
        return jnp.dot(hid, w2_ref[...], preferred_element_type=F32)

    @pl.when(j == 0)
    def _():
        gs = g_ref[...] * (1.0 + sc_ref[0])
        shift = sh_ref[0]
        half = x_ref.shape[0] // 2
        for r_lo in (0, half):
            _rms_modulate_rows(x_ref, h_ref, gs, shift, r_lo, r_lo + half)
            o_ref[r_lo:r_lo + half, :] = chunk(slice(r_lo, r_lo + half))

    last = pl.num_programs(1) - 1

    @pl.when(jnp.logical_and(j != 0, j != last))
    def _():
        o_ref[...] += chunk()

    @pl.when(j == last)
    def _():
        gf = gf_ref[0]
        gfin = gfin_ref[...]
        half = x_ref.shape[0] // 2
        for r_lo in (0, half):
            o_ref[r_lo:r_lo + half, :] += chunk(slice(r_lo, r_lo + half))
            for r0 in range(r_lo, r_lo + half, NORM_ROWS):
                rs = slice(r0, r0 + NORM_ROWS)
                x2 = x_ref[rs, :] + gf * o_ref[rs, :]
                ms = jnp.mean(x2 * x2, axis=-1, keepdims=True)
                o_ref[rs, :] = x2 * lax.rsqrt(ms + EPS) * gfin


def _ffn(x1, g, shift, scale, gate, w1, w2, g_final, seq):
    M, D = x1.shape
    F = w1.shape[1]
    tm, tf = 1024, 1024
    per_b = seq // tm
    bidx = lambda i, j: (i // per_b, 0, 0)
    return pl.pallas_call(
        _ffn_kernel,
        out_shape=jax.ShapeDtypeStruct((M, D), F32),
        grid=(M // tm, F // tf),
        in_specs=[
            pl.BlockSpec((tm, D), lambda i, j: (i, 0)),
            pl.BlockSpec((1, D), lambda i, j: (0, 0)),
            pl.BlockSpec((1, 1, D), bidx),
            pl.BlockSpec((1, 1, D), bidx),
            pl.BlockSpec((1, 1, D), bidx),
            pl.BlockSpec((D, tf), lambda i, j: (0, j)),
            pl.BlockSpec((tf, D), lambda i, j: (j, 0)),
            pl.BlockSpec((1, D), lambda i, j: (0, 0)),
        ],
        out_specs=pl.BlockSpec((tm, D), lambda i, j: (i, 0)),
        scratch_shapes=[pltpu.VMEM((tm, D), BF16)],
        compiler_params=_params(("parallel", "arbitrary"), vmem=FFN_VMEM_LIMIT),
        name="ffn",
    )(x1, g, shift, scale, gate, w1, w2, g_final)


def kernel(x, c, positions, w_ada, b_ada, g_norm_mix, w_in, conv_w, conv_b, conv_ln_g, conv_ln_b,
           w_conv_out, ret_gn_g, ret_gn_b, w_ret_out, w_out, g_norm_ffn, w_ff1, w_ff2, g_norm_final):
    B, S, D = x.shape
    assert w_ada.shape[0] == 1, "kernel supports DEPTH == 1"
    l = 0
    x2d = x.reshape(B * S, D)
    mod = _mod(c, w_ada[l], b_ada[l]).reshape(B, N_MOD, 1, D)
    shift_m, scale_m, gate_m, shift_f, scale_f, gate_f = (mod[:, i] for i in range(N_MOD))
    proj, y_conv, wc, wr, wo, w1, w2 = _inproj(
        x2d, g_norm_mix[l].reshape(1, D), shift_m, scale_m, w_in[l].astype(BF16),
        conv_w[l], conv_b[l], S,
        to_cast=(w_conv_out[l], w_ret_out[l], w_out[l], w_ff1[l], w_ff2[l]))
    a_ret = _ret(proj, positions, ret_gn_g[l], ret_gn_b[l], B, S)
    x1 = _mixout(y_conv, a_ret, proj, x2d, gate_m, conv_ln_g[l], conv_ln_b[l], wc, wr, wo, S)
    out = _ffn(x1, g_norm_ffn[l].reshape(1, D), shift_f, scale_f, gate_f, w1, w2,
               g_norm_final.reshape(1, D), S)
    return out.reshape(B, S, D)
```
